```python
import math, functools
import jax, jax.numpy as jnp
from jax import lax
import numpy as np

D_MODEL = 2048
BATCH = 16
SEQ = 2048
DEPTH = 1
DEC_BATCH = 16
DEC_SEQ = 64
PAST_LEN = 4096

CHUNK = 64
N_META = 16
ATTN_WIDTH = D_MODEL // 2
CONV_WIDTH = D_MODEL - ATTN_WIDTH
HEAD_DIM = 64
N_HEADS = ATTN_WIDTH // HEAD_DIM
N_KV_HEADS = max(1, N_HEADS // 8)
WINDOW = 128
N_BAND = WINDOW // CHUNK
ROPE_DIM = HEAD_DIM // 4
ROPE_THETA = 500000.0
CONV_K = 31
FFN_CONV_K = 3
D_FF = (11 * D_MODEL) // 4
EPS = 1e-6
Q_COLS = N_HEADS * HEAD_DIM
KV_COLS = N_KV_HEADS * HEAD_DIM
IN_COLS = Q_COLS + 2 * KV_COLS + 2 * CONV_WIDTH
NEG_BIG = -1e30

kernel_name = "hymba_stream_swa_sink_conformer_convffn"


def rms_norm(x, g):
    xf = x.astype(jnp.float32)
    y = xf * lax.rsqrt(jnp.mean(xf * xf, axis=-1, keepdims=True) + EPS)
    return (y * g.astype(jnp.float32)).astype(x.dtype)


def layer_norm(x, g, b):
    xf = x.astype(jnp.float32)
    mu = jnp.mean(xf, axis=-1, keepdims=True)
    xc = xf - mu
    y = xc * lax.rsqrt(jnp.mean(xc * xc, axis=-1, keepdims=True) + EPS)
    return (y * g.astype(jnp.float32) + b.astype(jnp.float32)).astype(x.dtype)


def apply_partial_rope(x, pos):
    half = ROPE_DIM // 2
    inv_freq = ROPE_THETA ** (-jnp.arange(half, dtype=jnp.float32) / half)
    ang = pos.astype(jnp.float32)[:, None] * inv_freq[None, :]
    cos = jnp.cos(ang)[:, None, :]
    sin = jnp.sin(ang)[:, None, :]
    xf = x.astype(jnp.float32)
    x1 = xf[..., :half]
    x2 = xf[..., half:ROPE_DIM]
    out = jnp.concatenate([x1 * cos - x2 * sin, x2 * cos + x1 * sin, xf[..., ROPE_DIM:]], axis=-1)
    return out.astype(x.dtype)


def causal_dwconv(x, hist, w, b):
    k = w.shape[0]
    c = x.shape[-1]
    xp = jnp.concatenate([hist.astype(x.dtype), x], axis=1)
    y = lax.conv_general_dilated(xp, w[:, None, :].astype(x.dtype), window_strides=(1,), padding='VALID',
                                 dimension_numbers=('NWC', 'WIO', 'NWC'), feature_group_count=c)
    return y + b.astype(x.dtype), xp[:, xp.shape[1] - (k - 1):]


def project_mixers(xn, w_in, q_norm_g, k_norm_g, pos):
    bsz, t, _ = xn.shape
    z = xn @ w_in
    q, k, v, ga, gb = jnp.split(z, [Q_COLS, Q_COLS + KV_COLS, Q_COLS + 2 * KV_COLS,
                                    Q_COLS + 2 * KV_COLS + CONV_WIDTH], axis=-1)
    q = apply_partial_rope(rms_norm(q.reshape(bsz, t, N_HEADS, HEAD_DIM), q_norm_g), pos)
    k = apply_partial_rope(rms_norm(k.reshape(bsz, t, N_KV_HEADS, HEAD_DIM), k_norm_g), pos)
    v = v.reshape(bsz, t, N_KV_HEADS, HEAD_DIM)
    u = ga * jax.nn.sigmoid(gb)
    return q, k, v, u


def sink_attention(q, k, v, sinks, valid):
    bsz, nb, nq = q.shape[:3]
    grp = N_HEADS // N_KV_HEADS
    qg = q.reshape(bsz, nb, nq, N_KV_HEADS, grp, HEAD_DIM).astype(jnp.float32)
    s = jnp.einsum('bnqkgd,bnskd->bnkgqs', qg, k.astype(jnp.float32)) * (HEAD_DIM ** -0.5)
    if valid is not None:
        s = jnp.where(valid[None, :, None, None, None, :], s, NEG_BIG)
    sink = sinks.astype(jnp.float32).reshape(N_KV_HEADS, grp)[None, None, :, :, None, None]
    m = jnp.maximum(jnp.max(s, axis=-1, keepdims=True), sink)
    p = jnp.exp(s - m)
    denom = jnp.sum(p, axis=-1, keepdims=True) + jnp.exp(sink - m)
    o = jnp.einsum('bnkgqs,bnskd->bnqkgd', p / denom, v.astype(jnp.float32))
    return o.reshape(bsz, nb, nq, N_HEADS * HEAD_DIM).astype(q.dtype)


def prompt_window_attention(q, k, v, sinks):
    bsz, seq_len = q.shape[:2]
    pad = CHUNK - N_META
    nblk = (seq_len + pad) // CHUNK
    qp = jnp.pad(q, ((0, 0), (pad, 0), (0, 0), (0, 0))).reshape(bsz, nblk, CHUNK, N_HEADS, HEAD_DIM)
    kp = jnp.pad(k, ((0, 0), (pad + WINDOW, 0), (0, 0), (0, 0))).reshape(bsz, nblk + N_BAND, CHUNK, N_KV_HEADS, HEAD_DIM)
    vp = jnp.pad(v, ((0, 0), (pad + WINDOW, 0), (0, 0), (0, 0))).reshape(bsz, nblk + N_BAND, CHUNK, N_KV_HEADS, HEAD_DIM)
    kb = jnp.concatenate([kp[:, j:j + nblk] for j in range(N_BAND + 1)], axis=2)
    vb = jnp.concatenate([vp[:, j:j + nblk] for j in range(N_BAND + 1)], axis=2)
    key_row = jnp.arange(nblk)[:, None] * CHUNK + jnp.arange((N_BAND + 1) * CHUNK)[None, :]
    valid = key_row >= pad + WINDOW
    o = sink_attention(qp, kb, vb, sinks, valid).reshape(bsz, nblk * CHUNK, ATTN_WIDTH)[:, pad:]
    return o, k[:, seq_len - WINDOW:], v[:, seq_len - WINDOW:]


def sample_window_attention(q, k, v, cache_k, cache_v, sinks):
    kc = jnp.concatenate([cache_k.astype(k.dtype), k], axis=1)
    vc = jnp.concatenate([cache_v.astype(v.dtype), v], axis=1)
    o = sink_attention(q[:, None], kc[:, None], vc[:, None], sinks, None)[:, 0]
    n = kc.shape[1]
    return o, kc[:, n - WINDOW:], vc[:, n - WINDOW:]


def trunk_layer(x, pos, attend, conv_hist, ffn_hist, norm1_g, w_in, q_norm_g, k_norm_g, conv_w, conv_b,
                conv_ln_g, conv_ln_b, w_out, norm2_g, w_gate, w_up, ffn_conv_w, ffn_conv_b, w_down):
    xn = rms_norm(x, norm1_g)
    q, k, v, u = project_mixers(xn, w_in, q_norm_g, k_norm_g, pos)
    a, k_state, v_state = attend(q, k, v)
    c, conv_state = causal_dwconv(u, conv_hist, conv_w, conv_b)
    c = jax.nn.silu(layer_norm(c, conv_ln_g, conv_ln_b))
    h = x + jnp.concatenate([a, c], axis=-1) @ w_out
    hn = rms_norm(h, norm2_g)
    g, ffn_state = causal_dwconv(hn @ w_gate, ffn_hist, ffn_conv_w, ffn_conv_b)
    y = h + (jax.nn.silu(g) * (hn @ w_up)) @ w_down
    return y, k_state, v_state, conv_state, ffn_state


def setup_inputs(seed: int = 0) -> dict:
    key = jax.random.key(seed)
    ks = jax.random.split(key, 24)
    f32 = jnp.float32
    nrm = lambda k, shape, s: jax.random.normal(k, shape, f32) * s
    return {
        "x_prompt": nrm(ks[0], (BATCH, SEQ, D_MODEL), 1.0),
        "x_sample": nrm(ks[1], (DEC_BATCH, DEC_SEQ, D_MODEL), 1.0),
        "cache_k": nrm(ks[2], (DEPTH, DEC_BATCH, WINDOW, N_KV_HEADS, HEAD_DIM), 1.0),
        "cache_v": nrm(ks[3], (DEPTH, DEC_BATCH, WINDOW, N_KV_HEADS, HEAD_DIM), 1.0),
        "state_conv": nrm(ks[4], (DEPTH, DEC_BATCH, CONV_K - 1, CONV_WIDTH), 0.5),
        "state_ffn_conv": nrm(ks[5], (DEPTH, DEC_BATCH, FFN_CONV_K - 1, D_FF), 1.0),
        "meta_tokens": nrm(ks[6], (N_META, D_MODEL), 1.0),
        "norm1_g": 1.0 + nrm(ks[7], (DEPTH, D_MODEL), 0.01),
        "w_in": nrm(ks[8], (DEPTH, D_MODEL, IN_COLS), D_MODEL ** -0.5),
        "q_norm_g": 1.0 + nrm(ks[9], (DEPTH, HEAD_DIM), 0.01),
        "k_norm_g": 1.0 + nrm(ks[10], (DEPTH, HEAD_DIM), 0.01),
        "sinks": nrm(ks[11], (DEPTH, N_HEADS), 0.5),
        "conv_w": nrm(ks[12], (DEPTH, CONV_K, CONV_WIDTH), CONV_K ** -0.5),
        "conv_b": nrm(ks[13], (DEPTH, CONV_WIDTH), 0.01),
        "conv_ln_g": 1.0 + nrm(ks[14], (DEPTH, CONV_WIDTH), 0.01),
        "conv_ln_b": nrm(ks[15], (DEPTH, CONV_WIDTH), 0.01),
        "w_out": nrm(ks[16], (DEPTH, ATTN_WIDTH + CONV_WIDTH, D_MODEL), (ATTN_WIDTH + CONV_WIDTH) ** -0.5),
        "norm2_g": 1.0 + nrm(ks[17], (DEPTH, D_MODEL), 0.01),
        "w_gate": nrm(ks[18], (DEPTH, D_MODEL, D_FF), D_MODEL ** -0.5),
        "w_up": nrm(ks[19], (DEPTH, D_MODEL, D_FF), D_MODEL ** -0.5),
        "ffn_conv_w": nrm(ks[20], (DEPTH, FFN_CONV_K, D_FF), FFN_CONV_K ** -0.5),
        "ffn_conv_b": nrm(ks[21], (DEPTH, D_FF), 0.01),
        "w_down": nrm(ks[22], (DEPTH, D_FF, D_MODEL), D_FF ** -0.5),
    }


def reference(x_prompt, x_sample, cache_k, cache_v, state_conv, state_ffn_conv, meta_tokens, norm1_g, w_in,
              q_norm_g, k_norm_g, sinks, conv_w, conv_b, conv_ln_g, conv_ln_b, w_out, norm2_g, w_gate, w_up,
              ffn_conv_w, ffn_conv_b, w_down):
    bsz, seq_len, _ = x_prompt.shape
    dbsz, dseq, _ = x_sample.shape
    x_p = jnp.concatenate([jnp.broadcast_to(meta_tokens[None].astype(x_prompt.dtype), (bsz, N_META, D_MODEL)), x_prompt], axis=1)
    pos_p = jnp.arange(N_META + seq_len, dtype=jnp.int32)
    x_s = x_sample
    pos_s = N_META + PAST_LEN + jnp.arange(dseq, dtype=jnp.int32)
    kp_l, vp_l, cp_l, fp_l, ks_l, vs_l, cs_l, fs_l = [], [], [], [], [], [], [], []
    for l in range(DEPTH):
        w = (norm1_g[l], w_in[l], q_norm_g[l], k_norm_g[l], conv_w[l], conv_b[l], conv_ln_g[l], conv_ln_b[l],
             w_out[l], norm2_g[l], w_gate[l], w_up[l], ffn_conv_w[l], ffn_conv_b[l], w_down[l])
        attend_p = functools.partial(prompt_window_attention, sinks=sinks[l])
        x_p, kp, vp, cp, fp = trunk_layer(
            x_p, pos_p, attend_p,
            jnp.zeros((bsz, CONV_K - 1, CONV_WIDTH), x_p.dtype),
            jnp.zeros((bsz, FFN_CONV_K - 1, D_FF), x_p.dtype), *w)
        attend_s = functools.partial(sample_window_attention, cache_k=cache_k[l], cache_v=cache_v[l], sinks=sinks[l])
        x_s, ks_, vs_, cs_, fs_ = trunk_layer(x_s, pos_s, attend_s, state_conv[l], state_ffn_conv[l], *w)
        kp_l.append(kp); vp_l.append(vp); cp_l.append(cp); fp_l.append(fp)
        ks_l.append(ks_); vs_l.append(vs_); cs_l.append(cs_); fs_l.append(fs_)
    y_prompt = x_p[:, N_META:]
    y_sample = x_s
    new_k_prompt = jnp.stack(kp_l, axis=0)
    new_v_prompt = jnp.stack(vp_l, axis=0)
    new_conv_prompt = jnp.stack(cp_l, axis=0)
    new_ffn_conv_prompt = jnp.stack(fp_l, axis=0)
    new_k_sample = jnp.stack(ks_l, axis=0)
    new_v_sample = jnp.stack(vs_l, axis=0)
    new_conv_sample = jnp.stack(cs_l, axis=0)
    new_ffn_conv_sample = jnp.stack(fs_l, axis=0)
    return (y_prompt, y_sample, new_k_prompt, new_v_prompt, new_conv_prompt, new_ffn_conv_prompt,
            new_k_sample, new_v_sample, new_conv_sample, new_ffn_conv_sample)
```

```python
import functools

import jax
import jax.numpy as jnp
from jax import lax
from jax.experimental import pallas as pl
from jax.experimental.pallas import tpu as pltpu

D_MODEL = 2048
N_META = 16
PAST_LEN = 4096
CHUNK = 64
HEAD_DIM = 64
N_HEADS = 16
N_KV_HEADS = 2
GROUP = N_HEADS // N_KV_HEADS
ATTN_WIDTH = N_HEADS * HEAD_DIM
CONV_WIDTH = D_MODEL - ATTN_WIDTH
KV_COLS = N_KV_HEADS * HEAD_DIM
WINDOW = 128
ROPE_HALF = 8
ROPE_THETA = 500000.0
CONV_K = 31
FFN_CONV_K = 3
D_FF = 5632
EPS = 1e-6
NEG_BIG = -1e30
IN_COLS = ATTN_WIDTH + 2 * KV_COLS + 2 * CONV_WIDTH

LANES = 128
KEY_SPAN = 256
CONV_HDR = 32
FFN_HDR = 8
FF_BLOCK = 512
VMEM_LIMIT = 56 * 1024 * 1024

_F32 = jnp.float32
_BF16 = jnp.bfloat16


def _dot(a, b):
    return jnp.dot(a, b, preferred_element_type=_F32)


def _head_norm_rope(xs, gain, rc, rsa, rsb):
    lo = lax.broadcasted_iota(jnp.int32, xs.shape, 1) < HEAD_DIM
    sq = xs * xs
    ss_lo = jnp.sum(jnp.where(lo, sq, 0.0), axis=-1, keepdims=True)
    ss_hi = jnp.sum(jnp.where(lo, 0.0, sq), axis=-1, keepdims=True)
    inv = jnp.where(lo, lax.rsqrt(ss_lo * (1.0 / HEAD_DIM) + EPS), lax.rsqrt(ss_hi * (1.0 / HEAD_DIM) + EPS))
    xn = xs * inv * gain
    return xn * rc + pltpu.roll(xn, ROPE_HALF, 1) * rsa + pltpu.roll(xn, LANES - ROPE_HALF, 1) * rsb


def _mixer_kernel(x_ref, kh_ref, vh_ref, uh_ref, rc_ref, rsa_ref, rsb_ref, sink_ref,
                  n1g_ref, win_ref, qg_ref, kg_ref, cw_ref, cb_ref, lng_ref, lnb_ref, wout_ref,
                  h_ref, kst_ref, vst_ref, ust_ref,
                  xn_buf, kctx, vctx, kbf, vdup, uctx, qhm, cpre, acb,
                  *, ns, seg, qb, carry, ninv_first):
    tm = ns * seg
    hdr = KEY_SPAN - qb
    j = pl.program_id(1)
    lo_lane = lax.broadcasted_iota(jnp.int32, (1, LANES), 1) < HEAD_DIM

    def dup_v(v, g):
        r = pltpu.roll(v, HEAD_DIM, 1)
        return jnp.where(lo_lane, v, r) if g == 0 else jnp.where(lo_lane, r, v)

    def load_history():
        kh = kh_ref[...]
        vh = vh_ref[...]
        kctx[:, hdr - WINDOW:hdr, :] = kh
        vctx[:, hdr - WINDOW:hdr, :] = vh
        kbf[:, hdr - WINDOW:hdr, :] = kh.astype(_BF16)
        vh2 = vh.reshape(ns * WINDOW, LANES)
        for g in range(N_KV_HEADS):
            vdup[g, :, hdr - WINDOW:hdr, :] = dup_v(vh2, g).reshape(ns, WINDOW, LANES).astype(_BF16)
        uctx[:, 0:CONV_HDR, :] = uh_ref[...]

    if carry:
        pl.when(j == 0)(load_history)
    else:
        load_history()
    if hdr > WINDOW:
        kbf[:, 0:hdr - WINDOW, :] = jnp.zeros((ns, hdr - WINDOW, LANES), _BF16)
        vdup[:, :, 0:hdr - WINDOW, :] = jnp.zeros((N_KV_HEADS, ns, hdr - WINDOW, LANES), _BF16)

    x = x_ref[...].reshape(tm, D_MODEL)
    ms = jnp.mean(x * x, axis=-1, keepdims=True)
    xn_buf[...] = (x * lax.rsqrt(ms + EPS) * n1g_ref[...]).astype(_BF16)

    rc = rc_ref[...]
    rsa = rsa_ref[...]
    rsb = rsb_ref[...]

    kv = _dot(xn_buf[...], win_ref[:, ATTN_WIDTH:ATTN_WIDTH + 2 * KV_COLS])
    k_new = _head_norm_rope(kv[:, 0:KV_COLS], kg_ref[...], rc, rsa, rsb)
    v_new = kv[:, KV_COLS:2 * KV_COLS]
    kctx[:, hdr:hdr + seg, :] = k_new.reshape(ns, seg, LANES)
    vctx[:, hdr:hdr + seg, :] = v_new.reshape(ns, seg, LANES)
    kbf[:, hdr:hdr + seg, :] = k_new.astype(_BF16).reshape(ns, seg, LANES)
    for g in range(N_KV_HEADS):
        vdup[g, :, hdr:hdr + seg, :] = dup_v(v_new, g).astype(_BF16).reshape(ns, seg, LANES)

    for nb in range(ATTN_WIDTH // 256):
        qq = _dot(xn_buf[...], win_ref[:, 256 * nb:256 * nb + 256])
        for half in range(2):
            m = 2 * nb + half
            g = (2 * m) // GROUP
            qr = _head_norm_rope(qq[:, LANES * half:LANES * (half + 1)], qg_ref[...], rc, rsa, rsb)
            qr = qr * (HEAD_DIM ** -0.5)
            sw = pltpu.roll(qr, HEAD_DIM, 1)
            if g == 0:
                even, odd = jnp.where(lo_lane, qr, 0.0), jnp.where(lo_lane, sw, 0.0)
            else:
                even, odd = jnp.where(lo_lane, 0.0, sw), jnp.where(lo_lane, 0.0, qr)
            qhm[2 * m] = even.astype(_BF16)
            qhm[2 * m + 1] = odd.astype(_BF16)

    glu0 = ATTN_WIDTH + 2 * KV_COLS
    for nb in range(CONV_WIDTH // 256):
        ga = _dot(xn_buf[...], win_ref[:, glu0 + 256 * nb:glu0 + 256 * nb + 256])
        gb = _dot(xn_buf[...], win_ref[:, glu0 + CONV_WIDTH + 256 * nb:glu0 + CONV_WIDTH + 256 * nb + 256])
        u = ga * jax.nn.sigmoid(gb)
        uctx[:, CONV_HDR:CONV_HDR + seg, 256 * nb:256 * nb + 256] = u.reshape(ns, seg, 256)

    rows = GROUP * qb
    col = lax.broadcasted_iota(jnp.int32, (rows, KEY_SPAN), 1)
    if qb == 2 * CHUNK:
        first_chunk = (lax.broadcasted_iota(jnp.int32, (rows, KEY_SPAN), 0) & (qb - 1)) < CHUNK
    for s in range(ns):
        for b in range(seg // qb):
            r0 = s * seg + b * qb
            if carry:
                ninv = jnp.where(j * (seg // qb) + b == 0, ninv_first, 0)
            else:
                ninv = ninv_first
            if qb == 2 * CHUNK:
                first_key = jnp.where(first_chunk, ninv, jnp.maximum(ninv, CHUNK))
                end_key = jnp.where(first_chunk, KEY_SPAN - CHUNK, KEY_SPAN)
                valid = (col >= first_key) & (col < end_key)
            else:
                valid = col >= ninv
            for g in range(N_KV_HEADS):
                q = qhm[GROUP * g:GROUP * (g + 1), r0:r0 + qb, :].reshape(rows, LANES)
                kc = kbf[s, b * qb:b * qb + KEY_SPAN, :]
                sc = lax.dot_general(q, kc, (((1,), (1,)), ((), ())), preferred_element_type=_F32)
                sc = jnp.where(valid, sc, NEG_BIG)
                sink = sink_ref[g]
                mx = jnp.maximum(jnp.max(sc, axis=-1, keepdims=True), sink)
                p = jnp.exp(sc - mx)
                denom = jnp.sum(p, axis=-1, keepdims=True) + jnp.exp(sink - mx)
                o = _dot(p.astype(_BF16), vdup[g, s, b * qb:b * qb + KEY_SPAN, :]) / denom
                for m4 in range(GROUP // 2):
                    slab = jnp.where(lo_lane, o[(2 * m4) * qb:(2 * m4 + 1) * qb], o[(2 * m4 + 1) * qb:(2 * m4 + 2) * qb])
                    c0 = LANES * ((GROUP // 2) * g + m4)
                    acb[r0:r0 + qb, c0:c0 + LANES] = slab.astype(_BF16)

    rb = min(64, seg)
    nrb = seg // rb
    for cv in range(CONV_WIDTH // LANES):
        lanes = slice(LANES * cv, LANES * (cv + 1))
        for s in range(ns):
            for i in range(nrb):
                first = i * rb + CONV_HDR - (CONV_K - 1)
                acc = jnp.broadcast_to(cb_ref[:, lanes], (rb, LANES))
                for k in range(CONV_K):
                    acc = acc + uctx[s, first + k:first + k + rb, lanes] * cw_ref[k:k + 1, lanes]
                cpre[s * seg + i * rb:s * seg + (i + 1) * rb, lanes] = acc

    for r in range(tm // rb):
        cp = cpre[r * rb:(r + 1) * rb, :]
        mu = jnp.mean(cp, axis=-1, keepdims=True)
        xc = cp - mu
        var = jnp.mean(xc * xc, axis=-1, keepdims=True)
        yn = xc * lax.rsqrt(var + EPS) * lng_ref[...] + lnb_ref[...]
        acb[r * rb:(r + 1) * rb, ATTN_WIDTH:D_MODEL] = (yn * jax.nn.sigmoid(yn)).astype(_BF16)

    for nb in range(D_MODEL // 512):
        cols = slice(512 * nb, 512 * (nb + 1))
        proj = _dot(acb[...], wout_ref[:, cols])
        h_ref[:, :, cols] = x_ref[:, :, cols] + proj.reshape(ns, seg, 512)

    kst_ref[...] = kctx[:, hdr + seg - WINDOW:hdr + seg, :]
    vst_ref[...] = vctx[:, hdr + seg - WINDOW:hdr + seg, :]
    ust_ref[...] = uctx[:, seg + CONV_HDR - (CONV_K - 1):seg + CONV_HDR, :]
    if carry:
        kctx[:, 0:WINDOW, :] = kctx[:, seg:seg + WINDOW, :]
        vctx[:, 0:WINDOW, :] = vctx[:, seg:seg + WINDOW, :]
        kbf[:, 0:WINDOW, :] = kbf[:, seg:seg + WINDOW, :]
        vdup[:, :, 0:WINDOW, :] = vdup[:, :, seg:seg + WINDOW, :]
        uctx[:, 0:CONV_HDR, :] = uctx[:, seg:seg + CONV_HDR, :]


def _const_spec(shape, single_buffer=False):
    zeros = (0,) * len(shape)
    if single_buffer:
        return pl.BlockSpec(shape, lambda b, j: zeros, pipeline_mode=pl.Buffered(1))
    return pl.BlockSpec(shape, lambda b, j: zeros)


def _mixer(x, khist, vhist, uhist, rope, sinks, w, *, ns, seg, qb, carry, ninv_first, hist_per_tile):
    n_seq, seq_len, _ = x.shape
    tm = ns * seg
    hdr = KEY_SPAN - qb
    grid = (n_seq // ns, seq_len // seg)
    hist_map = (lambda b, j: (b, 0, 0)) if hist_per_tile else (lambda b, j: (0, 0, 0))
    sink_cols = jnp.repeat(sinks.reshape(N_KV_HEADS, GROUP), qb, axis=1)[..., None].astype(_F32)
    rc, rsa, rsb = rope
    in_specs = [
        pl.BlockSpec((ns, seg, D_MODEL), lambda b, j: (b, j, 0)),
        pl.BlockSpec((ns, WINDOW, LANES), hist_map),
        pl.BlockSpec((ns, WINDOW, LANES), hist_map),
        pl.BlockSpec((ns, CONV_HDR, CONV_WIDTH), hist_map),
        pl.BlockSpec((tm, LANES), lambda b, j: (j, 0)),
        pl.BlockSpec((tm, LANES), lambda b, j: (j, 0)),
        pl.BlockSpec((tm, LANES), lambda b, j: (j, 0)),
        _const_spec((N_KV_HEADS, GROUP * qb, 1)),
        _const_spec((1, D_MODEL)),
        _const_spec((D_MODEL, IN_COLS), single_buffer=True),
        _const_spec((1, LANES)),
        _const_spec((1, LANES)),
        _const_spec((CONV_K, CONV_WIDTH)),
        _const_spec((1, CONV_WIDTH)),
        _const_spec((1, CONV_WIDTH)),
        _const_spec((1, CONV_WIDTH)),
        _const_spec((D_MODEL, D_MODEL), single_buffer=True),
    ]
    out_specs = [
        pl.BlockSpec((ns, seg, D_MODEL), lambda b, j: (b, j, 0)),
        pl.BlockSpec((ns, WINDOW, LANES), lambda b, j: (b, 0, 0)),
        pl.BlockSpec((ns, WINDOW, LANES), lambda b, j: (b, 0, 0)),
        pl.BlockSpec((ns, CONV_K - 1, CONV_WIDTH), lambda b, j: (b, 0, 0)),
    ]
    out_shape = [
        jax.ShapeDtypeStruct((n_seq, seq_len, D_MODEL), _F32),
        jax.ShapeDtypeStruct((n_seq, WINDOW, LANES), _F32),
        jax.ShapeDtypeStruct((n_seq, WINDOW, LANES), _F32),
        jax.ShapeDtypeStruct((n_seq, CONV_K - 1, CONV_WIDTH), _F32),
    ]
    scratch = [
        pltpu.VMEM((tm, D_MODEL), _BF16),
        pltpu.VMEM((ns, hdr + seg, LANES), _F32),
        pltpu.VMEM((ns, hdr + seg, LANES), _F32),
        pltpu.VMEM((ns, hdr + seg, LANES), _BF16),
        pltpu.VMEM((N_KV_HEADS, ns, hdr + seg, LANES), _BF16),
        pltpu.VMEM((ns, CONV_HDR + seg, CONV_WIDTH), _F32),
        pltpu.VMEM((N_HEADS, tm, LANES), _BF16),
        pltpu.VMEM((tm, CONV_WIDTH), _F32),
        pltpu.VMEM((tm, D_MODEL), _BF16),
    ]
    kern = functools.partial(_mixer_kernel, ns=ns, seg=seg, qb=qb, carry=carry, ninv_first=ninv_first)
    return pl.pallas_call(
        kern,
        grid=grid,
        in_specs=in_specs,
        out_specs=out_specs,
        out_shape=out_shape,
        scratch_shapes=scratch,
        compiler_params=pltpu.CompilerParams(
            dimension_semantics=("arbitrary", "arbitrary"), vmem_limit_bytes=VMEM_LIMIT),
        name=f"mixer_ns{ns}_seg{seg}",
    )(x, khist, vhist, uhist, rc, rsa, rsb, sink_cols,
      w["norm1_g"], w["w_in"], w["q_gain"], w["k_gain"], w["conv_w"], w["conv_b"],
      w["conv_ln_g"], w["conv_ln_b"], w["w_out"])


def _ffn_kernel(h_ref, fh_ref, n2g_ref, wg_ref, wu_ref, fcw_ref, fcb_ref, wd_ref,
                y_ref, fst_ref, hn_buf, g_buf, carry_buf, *, ns, seg, carry):
    tm = ns * seg
    j = pl.program_id(1)
    c = pl.program_id(2)

    @pl.when(c == 0)
    def _():
        h = h_ref[...].reshape(tm, D_MODEL)
        ms = jnp.mean(h * h, axis=-1, keepdims=True)
        hn_buf[...] = (h * lax.rsqrt(ms + EPS) * n2g_ref[...]).astype(_BF16)
        y_ref[...] = h_ref[...]
        g_buf[0:FFN_HDR, :] = jnp.zeros((FFN_HDR, FF_BLOCK), _F32)

    gp = _dot(hn_buf[...], wg_ref[...])
    up = _dot(hn_buf[...], wu_ref[...])
    g_buf[FFN_HDR:FFN_HDR + tm, :] = gp
    s1 = g_buf[FFN_HDR - 1:FFN_HDR - 1 + tm, :].reshape(ns, seg, FF_BLOCK)
    s2 = g_buf[FFN_HDR - 2:FFN_HDR - 2 + tm, :].reshape(ns, seg, FF_BLOCK)
    gp3 = gp.reshape(ns, seg, FF_BLOCK)
    if carry:
        @pl.when(j == 0)
        def _():
            carry_buf[c, FFN_HDR - 2:FFN_HDR, :] = fh_ref[0]

        h0 = carry_buf[c, FFN_HDR - 2:FFN_HDR - 1, :][None]
        h1 = carry_buf[c, FFN_HDR - 1:FFN_HDR, :][None]
    else:
        fh = fh_ref[...]
        h0 = fh[:, 0:1, :]
        h1 = fh[:, 1:2, :]
    rid = lax.broadcasted_iota(jnp.int32, (ns, seg, FF_BLOCK), 1)
    s1 = jnp.where(rid == 0, h1, s1)
    s2 = jnp.where(rid == 0, h0, jnp.where(rid == 1, h1, s2))
    gc = fcw_ref[0:1, :] * s2 + fcw_ref[1:2, :] * s1 + fcw_ref[2:3, :] * gp3 + fcb_ref[...]
    act = (gc * jax.nn.sigmoid(gc)) * up.reshape(ns, seg, FF_BLOCK)
    down = _dot(act.reshape(tm, FF_BLOCK).astype(_BF16), wd_ref[...])
    y_ref[...] += down.reshape(ns, seg, D_MODEL)
    fst_ref[...] = gp3[:, None, seg - (FFN_CONV_K - 1):seg, :]
    if carry:
        carry_buf[c, FFN_HDR - 2:FFN_HDR, :] = gp[tm - (FFN_CONV_K - 1):tm, :]


def _ffn(h, fhist, w, *, ns, seg, carry, hist_per_tile):
    n_seq, seq_len, _ = h.shape
    tm = ns * seg
    n_ff = D_FF // FF_BLOCK
    grid = (n_seq // ns, seq_len // seg, n_ff)
    fh_map = (lambda b, j, c: (b, 0, c)) if hist_per_tile else (lambda b, j, c: (0, 0, c))
    in_specs = [
        pl.BlockSpec((ns, seg, D_MODEL), lambda b, j, c: (b, j, 0)),
        pl.BlockSpec((ns, FFN_CONV_K - 1, FF_BLOCK), fh_map),
        pl.BlockSpec((1, D_MODEL), lambda b, j, c: (0, 0)),
        pl.BlockSpec((D_MODEL, FF_BLOCK), lambda b, j, c: (0, c)),
        pl.BlockSpec((D_MODEL, FF_BLOCK), lambda b, j, c: (0, c)),
        pl.BlockSpec((FFN_CONV_K, FF_BLOCK), lambda b, j, c: (0, c)),
        pl.BlockSpec((1, FF_BLOCK), lambda b, j, c: (0, c)),
        pl.BlockSpec((FF_BLOCK, D_MODEL), lambda b, j, c: (c, 0)),
    ]
    out_specs = [
        pl.BlockSpec((ns, seg, D_MODEL), lambda b, j, c: (b, j, 0)),
        pl.BlockSpec((ns, 1, FFN_CONV_K - 1, FF_BLOCK), lambda b, j, c: (b, j, 0, c)),
    ]
    out_shape = [
        jax.ShapeDtypeStruct((n_seq, seq_len, D_MODEL), _F32),
        jax.ShapeDtypeStruct((n_seq, seq_len // seg, FFN_CONV_K - 1, D_FF), _F32),
    ]
    scratch = [
        pltpu.VMEM((tm, D_MODEL), _BF16),
        pltpu.VMEM((FFN_HDR + tm, FF_BLOCK), _F32),
        pltpu.VMEM((n_ff, FFN_HDR, FF_BLOCK), _F32),
    ]
    kern = functools.partial(_ffn_kernel, ns=ns, seg=seg, carry=carry)
    return pl.pallas_call(
        kern,
        grid=grid,
        in_specs=in_specs,
        out_specs=out_specs,
        out_shape=out_shape,
        scratch_shapes=scratch,
        compiler_params=pltpu.CompilerParams(
            dimension_semantics=("arbitrary", "arbitrary", "arbitrary"), vmem_limit_bytes=VMEM_LIMIT),
        name=f"ffn_ns{ns}_seg{seg}",
    )(h, fhist, w["norm2_g"], w["w_gate"], w["w_up"], w["ffn_conv_w"], w["ffn_conv_b"], w["w_down"])


def _gate_rows_kernel(h_ref, n2g_ref, wg_ref, o_ref):
    h = h_ref[...]
    ms = jnp.mean(h * h, axis=-1, keepdims=True)
    hn = (h * lax.rsqrt(ms + EPS) * n2g_ref[...]).astype(_BF16)
    o_ref[...] = _dot(hn, wg_ref[...])


def _gate_rows(h, w):
    n = h.shape[0]
    return pl.pallas_call(
        _gate_rows_kernel,
        grid=(D_FF // FF_BLOCK,),
        in_specs=[
            pl.BlockSpec((n, D_MODEL), lambda c: (0, 0)),
            pl.BlockSpec((1, D_MODEL), lambda c: (0, 0)),
            pl.BlockSpec((D_MODEL, FF_BLOCK), lambda c: (0, c)),
        ],
        out_specs=pl.BlockSpec((n, FF_BLOCK), lambda c: (0, c)),
        out_shape=jax.ShapeDtypeStruct((n, D_FF), _F32),
        compiler_params=pltpu.CompilerParams(dimension_semantics=("arbitrary",)),
        name="gate_rows",
    )(h, w["norm2_g"], w["w_gate"])


def _rope_tables(pos, reps):
    inv_freq = ROPE_THETA ** (-jnp.arange(ROPE_HALF, dtype=_F32) / ROPE_HALF)
    ang = pos.astype(_F32)[:, None] * inv_freq[None, :]
    cos, sin = jnp.cos(ang), jnp.sin(ang)
    n = pos.shape[0]
    rest = HEAD_DIM - 2 * ROPE_HALF
    c = jnp.concatenate([cos, cos, jnp.ones((n, rest), _F32)], axis=1)
    sa = jnp.concatenate([jnp.zeros((n, ROPE_HALF), _F32), sin, jnp.zeros((n, rest), _F32)], axis=1)
    sb = jnp.concatenate([-sin, jnp.zeros((n, ROPE_HALF + rest), _F32)], axis=1)
    return tuple(jnp.tile(t, (reps, LANES // HEAD_DIM)) for t in (c, sa, sb))


def kernel(x_prompt, x_sample, cache_k, cache_v, state_conv, state_ffn_conv, meta_tokens, norm1_g, w_in,
           q_norm_g, k_norm_g, sinks, conv_w, conv_b, conv_ln_g, conv_ln_b, w_out, norm2_g, w_gate, w_up,
           ffn_conv_w, ffn_conv_b, w_down):
    assert w_in.shape[0] == 1, "single-layer problem"
    bsz, seq_len, _ = x_prompt.shape
    dbsz, dseq, _ = x_sample.shape
    lane_pair = LANES // HEAD_DIM
    w = {
        "norm1_g": norm1_g[0][None],
        "w_in": w_in[0].astype(_BF16),
        "q_gain": jnp.tile(q_norm_g[0], lane_pair)[None],
        "k_gain": jnp.tile(k_norm_g[0], lane_pair)[None],
        "conv_w": conv_w[0],
        "conv_b": conv_b[0][None],
        "conv_ln_g": conv_ln_g[0][None],
        "conv_ln_b": conv_ln_b[0][None],
        "w_out": w_out[0].astype(_BF16),
        "norm2_g": norm2_g[0][None],
        "w_gate": w_gate[0].astype(_BF16),
        "w_up": w_up[0].astype(_BF16),
        "ffn_conv_w": ffn_conv_w[0],
        "ffn_conv_b": ffn_conv_b[0][None],
        "w_down": w_down[0].astype(_BF16),
    }
    sink = sinks[0]

    zk = jnp.zeros((1, WINDOW, LANES), _F32)
    zu = jnp.zeros((1, CONV_HDR, CONV_WIDTH), _F32)
    h_m, k_m, v_m, u_m = _mixer(
        meta_tokens[None], zk, zk, zu, _rope_tables(jnp.arange(N_META, dtype=jnp.int32), 1), sink, w,
        ns=1, seg=N_META, qb=N_META, carry=False, ninv_first=KEY_SPAN - N_META, hist_per_tile=False)
    gate_m = _gate_rows(h_m[0], w)
    fhist_p = gate_m[N_META - (FFN_CONV_K - 1):][None]
    uhist_p = jnp.pad(u_m, ((0, 0), (CONV_HDR - (CONV_K - 1), 0), (0, 0)))

    rope_p = _rope_tables(N_META + jnp.arange(seq_len, dtype=jnp.int32), 1)
    h_p, k_p, v_p, u_p = _mixer(
        x_prompt, k_m, v_m, uhist_p, rope_p, sink, w,
        ns=1, seg=256, qb=2 * CHUNK, carry=True, ninv_first=WINDOW - N_META, hist_per_tile=False)
    y_p, f_p = _ffn(h_p, fhist_p, w, ns=1, seg=512, carry=True, hist_per_tile=False)

    ns_s = 4
    rope_s = _rope_tables(N_META + PAST_LEN + jnp.arange(dseq, dtype=jnp.int32), ns_s)
    uhist_s = jnp.pad(state_conv[0], ((0, 0), (CONV_HDR - (CONV_K - 1), 0), (0, 0)))
    h_s, k_s, v_s, u_s = _mixer(
        x_sample, cache_k[0].reshape(dbsz, WINDOW, LANES), cache_v[0].reshape(dbsz, WINDOW, LANES), uhist_s,
        rope_s, sink, w,
        ns=ns_s, seg=dseq, qb=dseq, carry=False, ninv_first=KEY_SPAN - WINDOW - dseq, hist_per_tile=True)
    y_s, f_s = _ffn(h_s, state_ffn_conv[0], w, ns=8, seg=dseq, carry=False, hist_per_tile=True)

    kv_shape_p = (1, bsz, WINDOW, N_KV_HEADS, HEAD_DIM)
    kv_shape_s = (1, dbsz, WINDOW, N_KV_HEADS, HEAD_DIM)
    return (y_p, y_s, k_p.reshape(kv_shape_p), v_p.reshape(kv_shape_p), u_p[None], f_p[None, :, -1],
            k_s.reshape(kv_shape_s), v_s.reshape(kv_shape_s), u_s[None], f_s[None, :, -1])
```

```python
import functools

import jax
import jax.numpy as jnp
from jax import lax
from jax.experimental import pallas as pl
from jax.experimental.pallas import tpu as pltpu

D_MODEL = 2048
N_META = 16
PAST_LEN = 4096
CHUNK = 64
HEAD_DIM = 64
N_HEADS = 16
N_KV_HEADS = 2
GROUP = N_HEADS // N_KV_HEADS
ATTN_WIDTH = N_HEADS * HEAD_DIM
CONV_WIDTH = D_MODEL - ATTN_WIDTH
KV_COLS = N_KV_HEADS * HEAD_DIM
WINDOW = 128
ROPE_HALF = 8
ROPE_THETA = 500000.0
CONV_K = 31
FFN_CONV_K = 3
D_FF = 5632
EPS = 1e-6
NEG_BIG = -1e30
IN_COLS = ATTN_WIDTH + 2 * KV_COLS + 2 * CONV_WIDTH

LANES = 128
SUBLANES = 8
KEY_SPAN = 256
CONV_HDR = 32
CONV_ROWS = 128
FFN_HDR = 8
FF_BLOCK = 512
VMEM_LIMIT = 56 * 1024 * 1024

_F32 = jnp.float32
_BF16 = jnp.bfloat16


def _dot(a, b):
    return jnp.dot(a, b, preferred_element_type=_F32)


def _head_norm_rope(xs, gain, rc, rsa, rsb):
    lo = lax.broadcasted_iota(jnp.int32, xs.shape, 1) < HEAD_DIM
    sq = xs * xs
    ss_lo = jnp.sum(jnp.where(lo, sq, 0.0), axis=-1, keepdims=True)
    ss_hi = jnp.sum(jnp.where(lo, 0.0, sq), axis=-1, keepdims=True)
    inv = jnp.where(lo, lax.rsqrt(ss_lo * (1.0 / HEAD_DIM) + EPS), lax.rsqrt(ss_hi * (1.0 / HEAD_DIM) + EPS))
    xn = xs * inv * gain
    return xn * rc + pltpu.roll(xn, ROPE_HALF, 1) * rsa + pltpu.roll(xn, LANES - ROPE_HALF, 1) * rsb


def _rows(ref, t0, n, seg, cols=slice(None)):
    if seg >= n:
        return ref[t0 // seg, t0 % seg:t0 % seg + n, cols]
    v = ref[t0 // seg:(t0 + n) // seg, :, cols]
    return v.reshape(n, v.shape[-1])


def _mixer_kernel(x_ref, kh_ref, vh_ref, uh_ref, rc_ref, rsa_ref, rsb_ref, sink_ref,
                  n1g_ref, win_ref, qg_ref, kg_ref, cw_ref, cb_ref, lng_ref, lnb_ref, wout_ref,
                  h_ref, kst_ref, vst_ref, ust_ref,
                  xn_buf, kctx, vctx, kbf, vdup, uctx, qhm, cpre, acb,
                  *, ns, seg, qb, br, carry, ninv_first):
    tm = ns * seg
    hdr = KEY_SPAN - qb
    pn = min(seg, br)
    j = pl.program_id(1)
    lo_lane = lax.broadcasted_iota(jnp.int32, (1, LANES), 1) < HEAD_DIM

    def dup_v(v, g):
        r = pltpu.roll(v, HEAD_DIM, 1)
        return jnp.where(lo_lane, v, r) if g == 0 else jnp.where(lo_lane, r, v)

    def load_history():
        kh = kh_ref[...]
        vh = vh_ref[...]
        kctx[:, hdr - WINDOW:hdr, :] = kh
        vctx[:, hdr - WINDOW:hdr, :] = vh
        kbf[:, hdr - WINDOW:hdr, :] = kh.astype(_BF16)
        vh2 = vh.reshape(ns * WINDOW, LANES)
        for g in range(N_KV_HEADS):
            vdup[g, :, hdr - WINDOW:hdr, :] = dup_v(vh2, g).reshape(ns, WINDOW, LANES).astype(_BF16)
        uctx[:, 0:CONV_HDR, :] = uh_ref[...]

    if carry:
        pl.when(j == 0)(load_history)
    else:
        load_history()
    if hdr > WINDOW:
        kbf[:, 0:hdr - WINDOW, :] = jnp.zeros((ns, hdr - WINDOW, LANES), _BF16)
        vdup[:, :, 0:hdr - WINDOW, :] = jnp.zeros((N_KV_HEADS, ns, hdr - WINDOW, LANES), _BF16)

    rows = GROUP * qb
    col = lax.broadcasted_iota(jnp.int32, (rows, KEY_SPAN), 1)
    if qb == 2 * CHUNK:
        first_chunk = (lax.broadcasted_iota(jnp.int32, (rows, KEY_SPAN), 0) & (qb - 1)) < CHUNK
    glu0 = ATTN_WIDTH + 2 * KV_COLS

    def pieces_of(t0):
        return [((t0 + i * pn) // seg, (t0 + i * pn) % seg) for i in range(br // pn)]

    def project_items(blk):
        t0 = blk * br
        tr = slice(t0, t0 + br)
        pieces = pieces_of(t0)

        def norm():
            x = _rows(x_ref, t0, br, seg)
            ms = jnp.mean(x * x, axis=-1, keepdims=True)
            xn_buf[...] = (x * lax.rsqrt(ms + EPS) * n1g_ref[...]).astype(_BF16)

        def keys_values():
            kv = _dot(xn_buf[...], win_ref[:, ATTN_WIDTH:ATTN_WIDTH + 2 * KV_COLS])
            k_new = _head_norm_rope(kv[:, 0:KV_COLS], kg_ref[...], rc_ref[tr, :], rsa_ref[tr, :], rsb_ref[tr, :])
            v_new = kv[:, KV_COLS:2 * KV_COLS]
            k_bf = k_new.astype(_BF16)
            v_dup = [dup_v(v_new, g).astype(_BF16) for g in range(N_KV_HEADS)]
            for i, (s, r) in enumerate(pieces):
                pr = slice(i * pn, (i + 1) * pn)
                kctx[s, hdr + r:hdr + r + pn, :] = k_new[pr]
                vctx[s, hdr + r:hdr + r + pn, :] = v_new[pr]
                kbf[s, hdr + r:hdr + r + pn, :] = k_bf[pr]
                for g in range(N_KV_HEADS):
                    vdup[g, s, hdr + r:hdr + r + pn, :] = v_dup[g][pr]

        def queries(nb):
            qq = _dot(xn_buf[...], win_ref[:, 256 * nb:256 * nb + 256])
            for half in range(2):
                m = 2 * nb + half
                g = (2 * m) // GROUP
                qr = _head_norm_rope(qq[:, LANES * half:LANES * (half + 1)], qg_ref[...],
                                     rc_ref[tr, :], rsa_ref[tr, :], rsb_ref[tr, :])
                qr = qr * (HEAD_DIM ** -0.5)
                sw = pltpu.roll(qr, HEAD_DIM, 1)
                if g == 0:
                    even, odd = jnp.where(lo_lane, qr, 0.0), jnp.where(lo_lane, sw, 0.0)
                else:
                    even, odd = jnp.where(lo_lane, 0.0, sw), jnp.where(lo_lane, 0.0, qr)
                qhm[2 * m, tr, :] = even.astype(_BF16)
                qhm[2 * m + 1, tr, :] = odd.astype(_BF16)

        def glu(nb):
            ga = _dot(xn_buf[...], win_ref[:, glu0 + 256 * nb:glu0 + 256 * nb + 256])
            gb = _dot(xn_buf[...], win_ref[:, glu0 + CONV_WIDTH + 256 * nb:glu0 + CONV_WIDTH + 256 * nb + 256])
            u = ga * jax.nn.sigmoid(gb)
            for i, (s, r) in enumerate(pieces):
                uctx[s, CONV_HDR + r:CONV_HDR + r + pn, 256 * nb:256 * nb + 256] = u[i * pn:(i + 1) * pn]

        return dict(
            first=[norm, keys_values] + [functools.partial(queries, nb) for nb in range(ATTN_WIDTH // 256)],
            glu=[functools.partial(glu, nb) for nb in range(CONV_WIDTH // 256)])

    def mix_items(blk):
        t0 = blk * br
        tr = slice(t0, t0 + br)
        pieces = pieces_of(t0)

        probs = {}

        def scores(i, b, g):
            s, r = pieces[i]
            q0 = t0 + i * pn + b * qb
            k0 = r + b * qb
            if carry:
                ninv = jnp.where(j * (seg // qb) + k0 // qb == 0, ninv_first, 0)
            else:
                ninv = ninv_first
            if qb == 2 * CHUNK:
                first_key = jnp.where(first_chunk, ninv, jnp.maximum(ninv, CHUNK))
                end_key = jnp.where(first_chunk, KEY_SPAN - CHUNK, KEY_SPAN)
                valid = (col >= first_key) & (col < end_key)
            else:
                valid = col >= ninv
            q = qhm[GROUP * g:GROUP * (g + 1), q0:q0 + qb, :].reshape(rows, LANES)
            kc = kbf[s, k0:k0 + KEY_SPAN, :]
            sc = lax.dot_general(q, kc, (((1,), (1,)), ((), ())), preferred_element_type=_F32)
            sc = jnp.where(valid, sc, NEG_BIG)
            sink = sink_ref[g]
            mx = jnp.maximum(jnp.max(sc, axis=-1, keepdims=True), sink)
            p = jnp.exp(sc - mx)
            denom = jnp.sum(p, axis=-1, keepdims=True) + jnp.exp(sink - mx)
            probs[i, b, g] = (p.astype(_BF16), denom)

        def values(i, b, g):
            s, r = pieces[i]
            q0 = t0 + i * pn + b * qb
            k0 = r + b * qb
            p, denom = probs[i, b, g]
            o = _dot(p, vdup[g, s, k0:k0 + KEY_SPAN, :]) / denom
            for m4 in range(GROUP // 2):
                slab = jnp.where(lo_lane, o[(2 * m4) * qb:(2 * m4 + 1) * qb], o[(2 * m4 + 1) * qb:(2 * m4 + 2) * qb])
                c0 = LANES * ((GROUP // 2) * g + m4)
                acb[q0:q0 + qb, c0:c0 + LANES] = slab.astype(_BF16)

        def conv(cv):
            lanes = slice(LANES * cv, LANES * (cv + 1))
            cn = min(pn, CONV_ROWS)
            for i, (s, r0) in enumerate(pieces):
                for sub in range(pn // cn):
                    r = r0 + sub * cn
                    acc = jnp.broadcast_to(cb_ref[:, lanes], (cn, LANES))
                    for back in range(SUBLANES):
                        offs = [o for o in range(CONV_HDR - CONV_K + 1, CONV_HDR + 1) if (o + back) % SUBLANES == 0]
                        lead = SUBLANES if back else 0
                        part = None
                        for o in offs:
                            a8 = o + back
                            term = uctx[s, r + a8 - lead:r + a8 + cn, lanes] * cw_ref[o - 2:o - 1, lanes]
                            part = term if part is None else part + term
                        acc = acc + part[lead - back:lead - back + cn]
                    c0 = i * pn + sub * cn
                    cpre[c0:c0 + cn, lanes] = acc

        def layer_norm_silu():
            cp = cpre[...]
            mu = jnp.mean(cp, axis=-1, keepdims=True)
            xc = cp - mu
            var = jnp.mean(xc * xc, axis=-1, keepdims=True)
            yn = xc * lax.rsqrt(var + EPS) * lng_ref[...] + lnb_ref[...]
            acb[tr, ATTN_WIDTH:D_MODEL] = (yn * jax.nn.sigmoid(yn)).astype(_BF16)

        def project_out(nb):
            cols = slice(512 * nb, 512 * (nb + 1))
            hv = _rows(x_ref, t0, br, seg, cols) + _dot(acb[tr, :], wout_ref[:, cols])
            if seg >= br:
                h_ref[t0 // seg, t0 % seg:t0 % seg + br, cols] = hv
            else:
                h_ref[t0 // seg:(t0 + br) // seg, :, cols] = hv.reshape(br // seg, seg, 512)

        blocks = [(i, b, g) for i in range(len(pieces)) for b in range(pn // qb) for g in range(N_KV_HEADS)]
        return dict(
            scores=[functools.partial(scores, *k) for k in blocks],
            values=[functools.partial(values, *k) for k in blocks],
            vector=[functools.partial(conv, cv) for cv in range(CONV_WIDTH // LANES)] + [layer_norm_silu],
            out=[functools.partial(project_out, nb) for nb in range(D_MODEL // 512)])

    nblk = tm // br
    stage = {}
    for it in range(nblk + 2):
        matmul_steps, vector_steps = [], []
        if 0 <= it - 1 < nblk:
            stage[it - 1] = mix_items(it - 1)
            matmul_steps += stage[it - 1]["scores"]
            vector_steps = stage[it - 1]["vector"]
        if 0 <= it - 2 < nblk:
            matmul_steps += stage.pop(it - 2)["out"]
        proj = project_items(it) if it < nblk else None
        if proj:
            matmul_steps += proj["first"]
        if 0 <= it - 1 < nblk:
            matmul_steps += stage[it - 1]["values"]
        if proj:
            matmul_steps += proj["glu"]
        for n in range(max(len(matmul_steps), len(vector_steps))):
            if n < len(matmul_steps):
                matmul_steps[n]()
            if n < len(vector_steps):
                vector_steps[n]()

    kst_ref[...] = kctx[:, hdr + seg - WINDOW:hdr + seg, :]
    vst_ref[...] = vctx[:, hdr + seg - WINDOW:hdr + seg, :]
    ust_ref[...] = uctx[:, seg + CONV_HDR - (CONV_K - 1):seg + CONV_HDR, :]
    if carry:
        kctx[:, 0:WINDOW, :] = kctx[:, seg:seg + WINDOW, :]
        vctx[:, 0:WINDOW, :] = vctx[:, seg:seg + WINDOW, :]
        kbf[:, 0:WINDOW, :] = kbf[:, seg:seg + WINDOW, :]
        vdup[:, :, 0:WINDOW, :] = vdup[:, :, seg:seg + WINDOW, :]
        uctx[:, 0:CONV_HDR, :] = uctx[:, seg:seg + CONV_HDR, :]


def _const_spec(shape, single_buffer=False):
    zeros = (0,) * len(shape)
    if single_buffer:
        return pl.BlockSpec(shape, lambda b, j: zeros, pipeline_mode=pl.Buffered(1))
    return pl.BlockSpec(shape, lambda b, j: zeros)


def _mixer(x, khist, vhist, uhist, rope, sinks, w, *, ns, seg, qb, br, carry, ninv_first, hist_per_tile):
    n_seq, seq_len, _ = x.shape
    tm = ns * seg
    hdr = KEY_SPAN - qb
    grid = (n_seq // ns, seq_len // seg)
    hist_map = (lambda b, j: (b, 0, 0)) if hist_per_tile else (lambda b, j: (0, 0, 0))
    sink_cols = jnp.repeat(sinks.reshape(N_KV_HEADS, GROUP), qb, axis=1)[..., None].astype(_F32)
    rc, rsa, rsb = rope
    in_specs = [
        pl.BlockSpec((ns, seg, D_MODEL), lambda b, j: (b, j, 0)),
        pl.BlockSpec((ns, WINDOW, LANES), hist_map),
        pl.BlockSpec((ns, WINDOW, LANES), hist_map),
        pl.BlockSpec((ns, CONV_HDR, CONV_WIDTH), hist_map),
        pl.BlockSpec((tm, LANES), lambda b, j: (j, 0)),
        pl.BlockSpec((tm, LANES), lambda b, j: (j, 0)),
        pl.BlockSpec((tm, LANES), lambda b, j: (j, 0)),
        _const_spec((N_KV_HEADS, GROUP * qb, 1), single_buffer=True),
        _const_spec((1, D_MODEL)),
        _const_spec((D_MODEL, IN_COLS), single_buffer=True),
        _const_spec((1, LANES)),
        _const_spec((1, LANES)),
        _const_spec((CONV_K, CONV_WIDTH)),
        _const_spec((1, CONV_WIDTH)),
        _const_spec((1, CONV_WIDTH)),
        _const_spec((1, CONV_WIDTH)),
        _const_spec((D_MODEL, D_MODEL), single_buffer=True),
    ]
    out_specs = [
        pl.BlockSpec((ns, seg, D_MODEL), lambda b, j: (b, j, 0)),
        pl.BlockSpec((ns, WINDOW, LANES), lambda b, j: (b, 0, 0)),
        pl.BlockSpec((ns, WINDOW, LANES), lambda b, j: (b, 0, 0)),
        pl.BlockSpec((ns, CONV_K - 1, CONV_WIDTH), lambda b, j: (b, 0, 0)),
    ]
    out_shape = [
        jax.ShapeDtypeStruct((n_seq, seq_len, D_MODEL), _F32),
        jax.ShapeDtypeStruct((n_seq, WINDOW, LANES), _F32),
        jax.ShapeDtypeStruct((n_seq, WINDOW, LANES), _F32),
        jax.ShapeDtypeStruct((n_seq, CONV_K - 1, CONV_WIDTH), _F32),
    ]
    scratch = [
        pltpu.VMEM((br, D_MODEL), _BF16),
        pltpu.VMEM((ns, hdr + seg, LANES), _F32),
        pltpu.VMEM((ns, hdr + seg, LANES), _F32),
        pltpu.VMEM((ns, hdr + seg, LANES), _BF16),
        pltpu.VMEM((N_KV_HEADS, ns, hdr + seg, LANES), _BF16),
        pltpu.VMEM((ns, CONV_HDR + seg, CONV_WIDTH), _F32),
        pltpu.VMEM((N_HEADS, tm, LANES), _BF16),
        pltpu.VMEM((br, CONV_WIDTH), _F32),
        pltpu.VMEM((tm, D_MODEL), _BF16),
    ]
    kern = functools.partial(_mixer_kernel, ns=ns, seg=seg, qb=qb, br=br, carry=carry, ninv_first=ninv_first)
    return pl.pallas_call(
        kern,
        grid=grid,
        in_specs=in_specs,
        out_specs=out_specs,
        out_shape=out_shape,
        scratch_shapes=scratch,
        compiler_params=pltpu.CompilerParams(
            dimension_semantics=("arbitrary", "arbitrary"), vmem_limit_bytes=VMEM_LIMIT),
        name=f"mixer_ns{ns}_seg{seg}",
    )(x, khist, vhist, uhist, rc, rsa, rsb, sink_cols,
      w["norm1_g"], w["w_in"], w["q_gain"], w["k_gain"], w["conv_w"], w["conv_b"],
      w["conv_ln_g"], w["conv_ln_b"], w["w_out"])


def _ffn_kernel(h_ref, fh_ref, n2g_ref, wg_ref, wu_ref, fcw_ref, fcb_ref, wd_ref,
                y_ref, fst_ref, hn_buf, g_buf, carry_buf, *, ns, seg, cr, carry):
    tm = ns * seg
    j = pl.program_id(1)
    c = pl.program_id(2)

    @pl.when(c == 0)
    def _():
        h = h_ref[...].reshape(tm, D_MODEL)
        ms = jnp.mean(h * h, axis=-1, keepdims=True)
        hn_buf[...] = (h * lax.rsqrt(ms + EPS) * n2g_ref[...]).astype(_BF16)
        y_ref[...] = h_ref[...]

    if carry:
        @pl.when(j == 0)
        def _():
            carry_buf[c, FFN_HDR - 2:FFN_HDR, :] = fh_ref[0]

        g_buf[FFN_HDR - 2:FFN_HDR, :] = carry_buf[c, FFN_HDR - 2:FFN_HDR, :]
    else:
        g_buf[FFN_HDR - 2:FFN_HDR, :] = fh_ref[0]

    def project(r):
        t0 = r * cr
        hn = hn_buf[t0:t0 + cr, :]
        gp = _dot(hn, wg_ref[...])
        up = _dot(hn, wu_ref[...])
        g_buf[FFN_HDR + t0:FFN_HDR + t0 + cr, :] = gp
        return gp, up

    def finish(r, gp, up):
        t0 = r * cr
        s1 = g_buf[FFN_HDR + t0 - 1:FFN_HDR + t0 - 1 + cr, :]
        s2 = g_buf[FFN_HDR + t0 - 2:FFN_HDR + t0 - 2 + cr, :]
        if seg < cr:
            nsc = cr // seg
            fh = fh_ref[t0 // seg:t0 // seg + nsc]
            h0 = fh[:, 0:1, :]
            h1 = fh[:, 1:2, :]
            rid = lax.broadcasted_iota(jnp.int32, (nsc, seg, FF_BLOCK), 1)
            s1 = jnp.where(rid == 0, h1, s1.reshape(nsc, seg, FF_BLOCK)).reshape(cr, FF_BLOCK)
            s2 = jnp.where(rid == 0, h0, jnp.where(rid == 1, h1, s2.reshape(nsc, seg, FF_BLOCK))).reshape(cr, FF_BLOCK)
        gc = fcw_ref[0:1, :] * s2 + fcw_ref[1:2, :] * s1 + fcw_ref[2:3, :] * gp + fcb_ref[...]
        act = (gc * jax.nn.sigmoid(gc)) * up
        down = _dot(act.astype(_BF16), wd_ref[...])
        if seg >= cr:
            y_ref[t0 // seg, t0 % seg:t0 % seg + cr, :] += down
        else:
            y_ref[t0 // seg:(t0 + cr) // seg, :, :] += down.reshape(cr // seg, seg, D_MODEL)

    nchain = tm // cr
    nxt = project(0)
    for r in range(nchain):
        cur = nxt
        if r + 1 < nchain:
            nxt = project(r + 1)
        finish(r, *cur)

    last = g_buf[FFN_HDR:FFN_HDR + tm, :].reshape(ns, seg, FF_BLOCK)[:, seg - (FFN_CONV_K - 1):seg, :]
    fst_ref[...] = last[:, None]
    if carry:
        carry_buf[c, FFN_HDR - 2:FFN_HDR, :] = g_buf[FFN_HDR + tm - 2:FFN_HDR + tm, :]


def _ffn(h, fhist, w, *, ns, seg, cr, carry, hist_per_tile):
    n_seq, seq_len, _ = h.shape
    tm = ns * seg
    n_ff = D_FF // FF_BLOCK
    grid = (n_seq // ns, seq_len // seg, n_ff)
    fh_map = (lambda b, j, c: (b, 0, c)) if hist_per_tile else (lambda b, j, c: (0, 0, c))
    in_specs = [
        pl.BlockSpec((ns, seg, D_MODEL), lambda b, j, c: (b, j, 0)),
        pl.BlockSpec((ns, FFN_CONV_K - 1, FF_BLOCK), fh_map),
        pl.BlockSpec((1, D_MODEL), lambda b, j, c: (0, 0)),
        pl.BlockSpec((D_MODEL, FF_BLOCK), lambda b, j, c: (0, c)),
        pl.BlockSpec((D_MODEL, FF_BLOCK), lambda b, j, c: (0, c)),
        pl.BlockSpec((FFN_CONV_K, FF_BLOCK), lambda b, j, c: (0, c)),
        pl.BlockSpec((1, FF_BLOCK), lambda b, j, c: (0, c)),
        pl.BlockSpec((FF_BLOCK, D_MODEL), lambda b, j, c: (c, 0)),
    ]
    out_specs = [
        pl.BlockSpec((ns, seg, D_MODEL), lambda b, j, c: (b, j, 0)),
        pl.BlockSpec((ns, 1, FFN_CONV_K - 1, FF_BLOCK), lambda b, j, c: (b, j, 0, c)),
    ]
    out_shape = [
        jax.ShapeDtypeStruct((n_seq, seq_len, D_MODEL), _F32),
        jax.ShapeDtypeStruct((n_seq, seq_len // seg, FFN_CONV_K - 1, D_FF), _F32),
    ]
    scratch = [
        pltpu.VMEM((tm, D_MODEL), _BF16),
        pltpu.VMEM((FFN_HDR + tm, FF_BLOCK), _F32),
        pltpu.VMEM((n_ff, FFN_HDR, FF_BLOCK), _F32),
    ]
    kern = functools.partial(_ffn_kernel, ns=ns, seg=seg, cr=cr, carry=carry)
    return pl.pallas_call(
        kern,
        grid=grid,
        in_specs=in_specs,
        out_specs=out_specs,
        out_shape=out_shape,
        scratch_shapes=scratch,
        compiler_params=pltpu.CompilerParams(
            dimension_semantics=("arbitrary", "arbitrary", "arbitrary"), vmem_limit_bytes=VMEM_LIMIT),
        name=f"ffn_ns{ns}_seg{seg}",
    )(h, fhist, w["norm2_g"], w["w_gate"], w["w_up"], w["ffn_conv_w"], w["ffn_conv_b"], w["w_down"])


def _gate_rows_kernel(h_ref, n2g_ref, wg_ref, o_ref):
    h = h_ref[...]
    ms = jnp.mean(h * h, axis=-1, keepdims=True)
    hn = (h * lax.rsqrt(ms + EPS) * n2g_ref[...]).astype(_BF16)
    o_ref[...] = _dot(hn, wg_ref[...])


def _gate_rows(h, w):
    n = h.shape[0]
    return pl.pallas_call(
        _gate_rows_kernel,
        grid=(D_FF // FF_BLOCK,),
        in_specs=[
            pl.BlockSpec((n, D_MODEL), lambda c: (0, 0)),
            pl.BlockSpec((1, D_MODEL), lambda c: (0, 0)),
            pl.BlockSpec((D_MODEL, FF_BLOCK), lambda c: (0, c)),
        ],
        out_specs=pl.BlockSpec((n, FF_BLOCK), lambda c: (0, c)),
        out_shape=jax.ShapeDtypeStruct((n, D_FF), _F32),
        compiler_params=pltpu.CompilerParams(dimension_semantics=("arbitrary",)),
        name="gate_rows",
    )(h, w["norm2_g"], w["w_gate"])


def _rope_tables(pos, reps):
    inv_freq = ROPE_THETA ** (-jnp.arange(ROPE_HALF, dtype=_F32) / ROPE_HALF)
    ang = pos.astype(_F32)[:, None] * inv_freq[None, :]
    cos, sin = jnp.cos(ang), jnp.sin(ang)
    n = pos.shape[0]
    rest = HEAD_DIM - 2 * ROPE_HALF
    c = jnp.concatenate([cos, cos, jnp.ones((n, rest), _F32)], axis=1)
    sa = jnp.concatenate([jnp.zeros((n, ROPE_HALF), _F32), sin, jnp.zeros((n, rest), _F32)], axis=1)
    sb = jnp.concatenate([-sin, jnp.zeros((n, ROPE_HALF + rest), _F32)], axis=1)
    return tuple(jnp.tile(t, (reps, LANES // HEAD_DIM)) for t in (c, sa, sb))


def kernel(x_prompt, x_sample, cache_k, cache_v, state_conv, state_ffn_conv, meta_tokens, norm1_g, w_in,
           q_norm_g, k_norm_g, sinks, conv_w, conv_b, conv_ln_g, conv_ln_b, w_out, norm2_g, w_gate, w_up,
           ffn_conv_w, ffn_conv_b, w_down):
    assert w_in.shape[0] == 1, "single-layer problem"
    bsz, seq_len, _ = x_prompt.shape
    dbsz, dseq, _ = x_sample.shape
    lane_pair = LANES // HEAD_DIM
    w = {
        "norm1_g": norm1_g[0][None],
        "w_in": w_in[0].astype(_BF16),
        "q_gain": jnp.tile(q_norm_g[0], lane_pair)[None],
        "k_gain": jnp.tile(k_norm_g[0], lane_pair)[None],
        "conv_w": conv_w[0],
        "conv_b": conv_b[0][None],
        "conv_ln_g": conv_ln_g[0][None],
        "conv_ln_b": conv_ln_b[0][None],
        "w_out": w_out[0].astype(_BF16),
        "norm2_g": norm2_g[0][None],
        "w_gate": w_gate[0].astype(_BF16),
        "w_up": w_up[0].astype(_BF16),
        "ffn_conv_w": ffn_conv_w[0],
        "ffn_conv_b": ffn_conv_b[0][None],
        "w_down": w_down[0].astype(_BF16),
    }
    sink = sinks[0]

    zk = jnp.zeros((1, WINDOW, LANES), _F32)
    zu = jnp.zeros((1, CONV_HDR, CONV_WIDTH), _F32)
    h_m, k_m, v_m, u_m = _mixer(
        meta_tokens[None], zk, zk, zu, _rope_tables(jnp.arange(N_META, dtype=jnp.int32), 1), sink, w,
        ns=1, seg=N_META, qb=N_META, br=N_META, carry=False, ninv_first=KEY_SPAN - N_META, hist_per_tile=False)
    gate_m = _gate_rows(h_m[0], w)
    fhist_p = gate_m[N_META - (FFN_CONV_K - 1):][None]
    uhist_p = jnp.pad(u_m, ((0, 0), (CONV_HDR - (CONV_K - 1), 0), (0, 0)))

    rope_p = _rope_tables(N_META + jnp.arange(seq_len, dtype=jnp.int32), 1)
    h_p, k_p, v_p, u_p = _mixer(
        x_prompt, k_m, v_m, uhist_p, rope_p, sink, w,
        ns=1, seg=256, qb=2 * CHUNK, br=128, carry=True, ninv_first=WINDOW - N_META, hist_per_tile=False)
    y_p, f_p = _ffn(h_p, fhist_p, w, ns=1, seg=512, cr=256, carry=True, hist_per_tile=False)

    ns_s = 4
    rope_s = _rope_tables(N_META + PAST_LEN + jnp.arange(dseq, dtype=jnp.int32), ns_s)
    uhist_s = jnp.pad(state_conv[0], ((0, 0), (CONV_HDR - (CONV_K - 1), 0), (0, 0)))
    h_s, k_s, v_s, u_s = _mixer(
        x_sample, cache_k[0].reshape(dbsz, WINDOW, LANES), cache_v[0].reshape(dbsz, WINDOW, LANES), uhist_s,
        rope_s, sink, w,
        ns=ns_s, seg=dseq, qb=dseq, br=256, carry=False, ninv_first=KEY_SPAN - WINDOW - dseq, hist_per_tile=True)
    y_s, f_s = _ffn(h_s, state_ffn_conv[0], w, ns=8, seg=dseq, cr=256, carry=False, hist_per_tile=True)

    kv_shape_p = (1, bsz, WINDOW, N_KV_HEADS, HEAD_DIM)
    kv_shape_s = (1, dbsz, WINDOW, N_KV_HEADS, HEAD_DIM)
    return (y_p, y_s, k_p.reshape(kv_shape_p), v_p.reshape(kv_shape_p), u_p[None], f_p[None, :, -1],
            k_s.reshape(kv_shape_s), v_s.reshape(kv_shape_s), u_s[None], f_s[None, :, -1])
```

```python
import functools

import jax
import jax.numpy as jnp
from jax import lax
from jax.experimental import pallas as pl
from jax.experimental.pallas import tpu as pltpu

D_MODEL = 2048
N_META = 16
PAST_LEN = 4096
CHUNK = 64
HEAD_DIM = 64
N_HEADS = 16
N_KV_HEADS = 2
GROUP = N_HEADS // N_KV_HEADS
ATTN_WIDTH = N_HEADS * HEAD_DIM
CONV_WIDTH = D_MODEL - ATTN_WIDTH
KV_COLS = N_KV_HEADS * HEAD_DIM
WINDOW = 128
ROPE_HALF = 8
ROPE_THETA = 500000.0
CONV_K = 31
FFN_CONV_K = 3
D_FF = 5632
EPS = 1e-6
NEG_BIG = -1e30
IN_COLS = ATTN_WIDTH + 2 * KV_COLS + 2 * CONV_WIDTH

LANES = 128
SUBLANES = 8
KEY_SPAN = 256
CONV_HDR = 32
CONV_ROWS = 128
NORM_ROWS = 16
FFN_HDR = 8
FF_BLOCK = 512
VMEM_LIMIT = 60 * 1024 * 1024

_F32 = jnp.float32
_BF16 = jnp.bfloat16


def _dot(a, b):
    return jnp.dot(a, b, preferred_element_type=_F32)


def _head_norm_rope(xs, gain, rc, rsa, rsb):
    lo = lax.broadcasted_iota(jnp.int32, xs.shape, 1) < HEAD_DIM
    sq = xs * xs
    ss_lo = jnp.sum(jnp.where(lo, sq, 0.0), axis=-1, keepdims=True)
    ss_hi = jnp.sum(jnp.where(lo, 0.0, sq), axis=-1, keepdims=True)
    inv = jnp.where(lo, lax.rsqrt(ss_lo * (1.0 / HEAD_DIM) + EPS), lax.rsqrt(ss_hi * (1.0 / HEAD_DIM) + EPS))
    xn = xs * inv * gain
    return xn * rc + pltpu.roll(xn, ROPE_HALF, 1) * rsa + pltpu.roll(xn, LANES - ROPE_HALF, 1) * rsb


def _rows(ref, t0, n, seg, cols=slice(None)):
    if seg >= n:
        return ref[t0 // seg, t0 % seg:t0 % seg + n, cols]
    v = ref[t0 // seg:(t0 + n) // seg, :, cols]
    return v.reshape(n, v.shape[-1])


def _mixer_kernel(*refs, ns, seg, qb, br, carry, ahead, ninv_first):
    if ahead:
        (x_ref, xnext_ref, kh_ref, vh_ref, uh_ref, rc_ref, rsa_ref, rsb_ref, rcn_ref, rsan_ref, rsbn_ref, sink_ref,
         n1g_ref, win_ref, qg_ref, kg_ref, cw_ref, cb_ref, lng_ref, lnb_ref, wout_ref,
         h_ref, kst_ref, vst_ref, ust_ref, xn_buf, kctx, vctx, kbf, vdup, uctx, qhm, cpre, acb) = refs
    else:
        (x_ref, kh_ref, vh_ref, uh_ref, rc_ref, rsa_ref, rsb_ref, sink_ref,
         n1g_ref, win_ref, qg_ref, kg_ref, cw_ref, cb_ref, lng_ref, lnb_ref, wout_ref,
         h_ref, kst_ref, vst_ref, ust_ref, xn_buf, kctx, vctx, kbf, vdup, uctx, qhm, cpre, acb) = refs
    tm = ns * seg
    hdr = KEY_SPAN - qb
    pn = min(seg, br)
    j = pl.program_id(1)
    lo_lane = lax.broadcasted_iota(jnp.int32, (1, LANES), 1) < HEAD_DIM

    def dup_v(v, g):
        r = pltpu.roll(v, HEAD_DIM, 1)
        return jnp.where(lo_lane, v, r) if g == 0 else jnp.where(lo_lane, r, v)

    def load_history():
        kh = kh_ref[...]
        vh = vh_ref[...]
        kctx[:, hdr - WINDOW:hdr, :] = kh
        vctx[:, hdr - WINDOW:hdr, :] = vh
        kbf[:, hdr - WINDOW:hdr, :] = kh.astype(_BF16)
        vh2 = vh.reshape(ns * WINDOW, LANES)
        for g in range(N_KV_HEADS):
            vdup[g, :, hdr - WINDOW:hdr, :] = dup_v(vh2, g).reshape(ns, WINDOW, LANES).astype(_BF16)
        uctx[:, 0:CONV_HDR, :] = uh_ref[...]

    if carry:
        pl.when(j == 0)(load_history)
    else:
        load_history()
    if hdr > WINDOW:
        kbf[:, 0:hdr - WINDOW, :] = jnp.zeros((ns, hdr - WINDOW, LANES), _BF16)
        vdup[:, :, 0:hdr - WINDOW, :] = jnp.zeros((N_KV_HEADS, ns, hdr - WINDOW, LANES), _BF16)

    rows = GROUP * qb
    glu0 = ATTN_WIDTH + 2 * KV_COLS

    def pieces_of(t0):
        if t0 >= tm:
            return [(0, t0)]
        return [((t0 + i * pn) // seg, (t0 + i * pn) % seg) for i in range(br // pn)]

    def project_items(blk):
        t0 = blk * br
        tr = slice(t0, t0 + br)
        pieces = pieces_of(t0)

        nxt = t0 >= tm
        rope = (rcn_ref[...], rsan_ref[...], rsbn_ref[...]) if nxt else None

        def rope_rows():
            return rope if nxt else (rc_ref[tr, :], rsa_ref[tr, :], rsb_ref[tr, :])

        def norm():
            nr = min(pn, NORM_ROWS)
            for r0 in range(0, br, nr):
                x = xnext_ref[0, r0:r0 + nr, :] if nxt else _rows(x_ref, t0 + r0, nr, seg)
                ms = jnp.mean(x * x, axis=-1, keepdims=True)
                xn_buf[r0:r0 + nr, :] = (x * lax.rsqrt(ms + EPS) * n1g_ref[...]).astype(_BF16)

        def keys_values():
            kv = _dot(xn_buf[...], win_ref[:, ATTN_WIDTH:ATTN_WIDTH + 2 * KV_COLS])
            k_new = _head_norm_rope(kv[:, 0:KV_COLS], kg_ref[...], *rope_rows())
            v_new = kv[:, KV_COLS:2 * KV_COLS]
            k_bf = k_new.astype(_BF16)
            v_dup = [dup_v(v_new, g).astype(_BF16) for g in range(N_KV_HEADS)]
            for i, (s, r) in enumerate(pieces):
                pr = slice(i * pn, (i + 1) * pn)
                kctx[s, hdr + r:hdr + r + pn, :] = k_new[pr]
                vctx[s, hdr + r:hdr + r + pn, :] = v_new[pr]
                kbf[s, hdr + r:hdr + r + pn, :] = k_bf[pr]
                for g in range(N_KV_HEADS):
                    vdup[g, s, hdr + r:hdr + r + pn, :] = v_dup[g][pr]

        def queries(nb):
            qq = _dot(xn_buf[...], win_ref[:, 256 * nb:256 * nb + 256])
            for half in range(2):
                m = 2 * nb + half
                g = (2 * m) // GROUP
                qr = _head_norm_rope(qq[:, LANES * half:LANES * (half + 1)], qg_ref[...], *rope_rows())
                qr = qr * (HEAD_DIM ** -0.5)
                sw = pltpu.roll(qr, HEAD_DIM, 1)
                if g == 0:
                    even, odd = jnp.where(lo_lane, qr, 0.0), jnp.where(lo_lane, sw, 0.0)
                else:
                    even, odd = jnp.where(lo_lane, 0.0, sw), jnp.where(lo_lane, 0.0, qr)
                qhm[2 * m, tr, :] = even.astype(_BF16)
                qhm[2 * m + 1, tr, :] = odd.astype(_BF16)

        def glu(nb):
            ga = _dot(xn_buf[...], win_ref[:, glu0 + 256 * nb:glu0 + 256 * nb + 256])
            gb = _dot(xn_buf[...], win_ref[:, glu0 + CONV_WIDTH + 256 * nb:glu0 + CONV_WIDTH + 256 * nb + 256])
            u = ga * jax.nn.sigmoid(gb)
            for i, (s, r) in enumerate(pieces):
                uctx[s, CONV_HDR + r:CONV_HDR + r + pn, 256 * nb:256 * nb + 256] = u[i * pn:(i + 1) * pn]

        return dict(
            first=[norm, keys_values] + [functools.partial(queries, nb) for nb in range(ATTN_WIDTH // 256)],
            glu=[functools.partial(glu, nb) for nb in range(CONV_WIDTH // 256)])

    def mix_items(blk):
        t0 = blk * br
        tr = slice(t0, t0 + br)
        pieces = pieces_of(t0)

        probs = {}

        def scores(i, b, g):
            s, r = pieces[i]
            q0 = t0 + i * pn + b * qb
            k0 = r + b * qb
            if carry:
                ninv = jnp.where(j * (seg // qb) + k0 // qb == 0, ninv_first, 0)
            else:
                ninv = ninv_first
            col = lax.broadcasted_iota(jnp.int32, (1, KEY_SPAN), 1)
            if qb == 2 * CHUNK:
                first_chunk = (lax.broadcasted_iota(jnp.int32, (rows, 1), 0) & (qb - 1)) < CHUNK
                first_key = jnp.where(first_chunk, ninv, jnp.maximum(ninv, CHUNK))
                end_key = jnp.where(first_chunk, KEY_SPAN - CHUNK, KEY_SPAN)
                valid = (col >= first_key) & (col < end_key)
            else:
                valid = jnp.broadcast_to(col, (rows, KEY_SPAN)) >= ninv
            q = qhm[GROUP * g:GROUP * (g + 1), q0:q0 + qb, :].reshape(rows, LANES)
            kc = kbf[s, k0:k0 + KEY_SPAN, :]
            sc = lax.dot_general(q, kc, (((1,), (1,)), ((), ())), preferred_element_type=_F32)
            sc = jnp.where(valid, sc, NEG_BIG)
            sink = sink_ref[g]
            mx = jnp.maximum(jnp.max(sc, axis=-1, keepdims=True), sink)
            p = jnp.exp(sc - mx)
            denom = jnp.sum(p, axis=-1, keepdims=True) + jnp.exp(sink - mx)
            probs[i, b, g] = (p.astype(_BF16), denom)

        def values(i, b, g):
            s, r = pieces[i]
            q0 = t0 + i * pn + b * qb
            k0 = r + b * qb
            p, denom = probs[i, b, g]
            o = _dot(p, vdup[g, s, k0:k0 + KEY_SPAN, :]) / denom
            for m4 in range(GROUP // 2):
                slab = jnp.where(lo_lane, o[(2 * m4) * qb:(2 * m4 + 1) * qb], o[(2 * m4 + 1) * qb:(2 * m4 + 2) * qb])
                c0 = LANES * ((GROUP // 2) * g + m4)
                acb[q0:q0 + qb, c0:c0 + LANES] = slab.astype(_BF16)

        def conv(cv):
            lanes = slice(LANES * cv, LANES * (cv + 1))
            cn = min(pn, CONV_ROWS)
            for i, (s, r0) in enumerate(pieces):
                for sub in range(pn // cn):
                    r = r0 + sub * cn
                    acc = jnp.broadcast_to(cb_ref[:, lanes], (cn, LANES))
                    for back in range(SUBLANES):
                        offs = [o for o in range(CONV_HDR - CONV_K + 1, CONV_HDR + 1) if (o + back) % SUBLANES == 0]
                        lead = SUBLANES if back else 0
                        part = None
                        for o in offs:
                            a8 = o + back
                            term = uctx[s, r + a8 - lead:r + a8 + cn, lanes] * cw_ref[o - 2:o - 1, lanes]
                            part = term if part is None else part + term
                        acc = acc + part[lead - back:lead - back + cn]
                    c0 = i * pn + sub * cn
                    cpre[c0:c0 + cn, lanes] = acc

        def layer_norm_silu():
            cp = cpre[...]
            mu = jnp.mean(cp, axis=-1, keepdims=True)
            xc = cp - mu
            var = jnp.mean(xc * xc, axis=-1, keepdims=True)
            yn = xc * lax.rsqrt(var + EPS) * lng_ref[...] + lnb_ref[...]
            acb[tr, ATTN_WIDTH:D_MODEL] = (yn * jax.nn.sigmoid(yn)).astype(_BF16)

        def project_out(nb):
            cols = slice(512 * nb, 512 * (nb + 1))
            hv = _rows(x_ref, t0, br, seg, cols) + _dot(acb[tr, :], wout_ref[:, cols])
            if seg >= br:
                h_ref[t0 // seg, t0 % seg:t0 % seg + br, cols] = hv
            else:
                h_ref[t0 // seg:(t0 + br) // seg, :, cols] = hv.reshape(br // seg, seg, 512)

        blocks = [(i, b, g) for i in range(len(pieces)) for b in range(pn // qb) for g in range(N_KV_HEADS)]
        return dict(
            scores=[functools.partial(scores, *k) for k in blocks],
            values=[functools.partial(values, *k) for k in blocks],
            vector=[functools.partial(conv, cv) for cv in range(CONV_WIDTH // LANES)] + [layer_norm_silu],
            out=[functools.partial(project_out, nb) for nb in range(D_MODEL // 512)])

    nblk = tm // br
    if ahead:
        @pl.when((pl.program_id(0) == 0) & (j == 0))
        def _():
            steps = project_items(0)
            for step in steps["first"] + steps["glu"]:
                step()

    stage = {}
    for it in range(1 if ahead else 0, nblk + 2):
        if nblk == 1 and not ahead:
            break
        matmul_steps, vector_steps = [], []
        if 0 <= it - 1 < nblk:
            stage[it - 1] = mix_items(it - 1)
            matmul_steps += stage[it - 1]["scores"]
            vector_steps = stage[it - 1]["vector"]
        if 0 <= it - 2 < nblk:
            matmul_steps += stage.pop(it - 2)["out"]
        proj = project_items(it) if it < nblk or (ahead and it == nblk) else None
        if proj:
            matmul_steps += proj["first"]
        if 0 <= it - 1 < nblk:
            matmul_steps += stage[it - 1]["values"]
        if proj:
            matmul_steps += proj["glu"]
        for n in range(max(len(matmul_steps), len(vector_steps))):
            if n < len(matmul_steps):
                matmul_steps[n]()
            if n < len(vector_steps):
                vector_steps[n]()

    if nblk == 1 and not ahead:
        proj, mix = project_items(0), mix_items(0)
        proj["first"][0]()
        for nb, glu_step in enumerate(proj["glu"]):
            glu_step()
            per = (CONV_WIDTH // LANES) // len(proj["glu"])
            for step in mix["vector"][per * nb:per * (nb + 1)]:
                step()
        for step in proj["first"][1:] + mix["scores"] + mix["vector"][-1:] + mix["values"] + mix["out"]:
            step()

    kst_ref[...] = kctx[:, hdr + seg - WINDOW:hdr + seg, :]
    vst_ref[...] = vctx[:, hdr + seg - WINDOW:hdr + seg, :]
    ust_ref[...] = uctx[:, seg + CONV_HDR - (CONV_K - 1):seg + CONV_HDR, :]
    if carry:
        kctx[:, 0:WINDOW, :] = kctx[:, seg:seg + WINDOW, :]
        vctx[:, 0:WINDOW, :] = vctx[:, seg:seg + WINDOW, :]
        kbf[:, 0:WINDOW, :] = kbf[:, seg:seg + WINDOW, :]
        vdup[:, :, 0:WINDOW, :] = vdup[:, :, seg:seg + WINDOW, :]
        uctx[:, 0:CONV_HDR, :] = uctx[:, seg:seg + CONV_HDR, :]
    if ahead:
        kctx[:, hdr:hdr + br, :] = kctx[:, hdr + seg:hdr + seg + br, :]
        vctx[:, hdr:hdr + br, :] = vctx[:, hdr + seg:hdr + seg + br, :]
        kbf[:, hdr:hdr + br, :] = kbf[:, hdr + seg:hdr + seg + br, :]
        vdup[:, :, hdr:hdr + br, :] = vdup[:, :, hdr + seg:hdr + seg + br, :]
        uctx[:, CONV_HDR:CONV_HDR + br, :] = uctx[:, CONV_HDR + seg:CONV_HDR + seg + br, :]
        qhm[:, 0:br, :] = qhm[:, tm:tm + br, :]


def _const_spec(shape, single_buffer=False):
    zeros = (0,) * len(shape)
    if single_buffer:
        return pl.BlockSpec(shape, lambda b, j: zeros, pipeline_mode=pl.Buffered(1))
    return pl.BlockSpec(shape, lambda b, j: zeros)


def _mixer(x, khist, vhist, uhist, rope, sinks, w, *, ns, seg, qb, br, carry, ninv_first, hist_per_tile,
           ahead=False):
    n_seq, seq_len, _ = x.shape
    tm = ns * seg
    hdr = KEY_SPAN - qb
    grid = (n_seq // ns, seq_len // seg)
    hist_map = (lambda b, j: (b, 0, 0)) if hist_per_tile else (lambda b, j: (0, 0, 0))
    sink_cols = jnp.repeat(sinks.reshape(N_KV_HEADS, GROUP), qb, axis=1)[..., None].astype(_F32)
    rc, rsa, rsb = rope
    spare = br if ahead else 0
    nj = seq_len // seg

    def next_tile(b, j):
        flat = jnp.minimum(b * nj + j + 1, n_seq * nj - 1)
        return flat // nj, flat % nj

    def xnext_map(b, j):
        nb, njx = next_tile(b, j)
        return nb, njx * (seg // br), 0

    def rope_next_map(b, j):
        return next_tile(b, j)[1] * (seg // br), 0

    x_specs = [pl.BlockSpec((ns, seg, D_MODEL), lambda b, j: (b, j, 0))]
    rope_specs = [pl.BlockSpec((tm, LANES), lambda b, j: (j, 0))] * 3
    if ahead:
        x_specs.append(pl.BlockSpec((1, br, D_MODEL), xnext_map))
        rope_specs += [pl.BlockSpec((br, LANES), rope_next_map)] * 3
    in_specs = x_specs + [
        pl.BlockSpec((ns, WINDOW, LANES), hist_map),
        pl.BlockSpec((ns, WINDOW, LANES), hist_map),
        pl.BlockSpec((ns, CONV_HDR, CONV_WIDTH), hist_map),
    ] + rope_specs + [
        _const_spec((N_KV_HEADS, GROUP * qb, 1), single_buffer=True),
        _const_spec((1, D_MODEL)),
        _const_spec((D_MODEL, IN_COLS), single_buffer=True),
        _const_spec((1, LANES)),
        _const_spec((1, LANES)),
        _const_spec((CONV_K, CONV_WIDTH)),
        _const_spec((1, CONV_WIDTH)),
        _const_spec((1, CONV_WIDTH)),
        _const_spec((1, CONV_WIDTH)),
        _const_spec((D_MODEL, D_MODEL), single_buffer=True),
    ]
    out_specs = [
        pl.BlockSpec((ns, seg, D_MODEL), lambda b, j: (b, j, 0)),
        pl.BlockSpec((ns, WINDOW, LANES), lambda b, j: (b, 0, 0)),
        pl.BlockSpec((ns, WINDOW, LANES), lambda b, j: (b, 0, 0)),
        pl.BlockSpec((ns, CONV_K - 1, CONV_WIDTH), lambda b, j: (b, 0, 0)),
    ]
    out_shape = [
        jax.ShapeDtypeStruct((n_seq, seq_len, D_MODEL), _F32),
        jax.ShapeDtypeStruct((n_seq, WINDOW, LANES), _F32),
        jax.ShapeDtypeStruct((n_seq, WINDOW, LANES), _F32),
        jax.ShapeDtypeStruct((n_seq, CONV_K - 1, CONV_WIDTH), _F32),
    ]
    scratch = [
        pltpu.VMEM((br, D_MODEL), _BF16),
        pltpu.VMEM((ns, hdr + seg + spare, LANES), _F32),
        pltpu.VMEM((ns, hdr + seg + spare, LANES), _F32),
        pltpu.VMEM((ns, hdr + seg + spare, LANES), _BF16),
        pltpu.VMEM((N_KV_HEADS, ns, hdr + seg + spare, LANES), _BF16),
        pltpu.VMEM((ns, CONV_HDR + seg + spare, CONV_WIDTH), _F32),
        pltpu.VMEM((N_HEADS, tm + spare, LANES), _BF16),
        pltpu.VMEM((br, CONV_WIDTH), _F32),
        pltpu.VMEM((tm, D_MODEL), _BF16),
    ]
    kern = functools.partial(_mixer_kernel, ns=ns, seg=seg, qb=qb, br=br, carry=carry, ahead=ahead, ninv_first=ninv_first)
    return pl.pallas_call(
        kern,
        grid=grid,
        in_specs=in_specs,
        out_specs=out_specs,
        out_shape=out_shape,
        scratch_shapes=scratch,
        compiler_params=pltpu.CompilerParams(
            dimension_semantics=("arbitrary", "arbitrary"), vmem_limit_bytes=VMEM_LIMIT),
        name=f"mixer_ns{ns}_seg{seg}",
    )(*([x, x] if ahead else [x]), khist, vhist, uhist, *([rc, rsa, rsb] * (2 if ahead else 1)), sink_cols,
      w["norm1_g"], w["w_in"], w["q_gain"], w["k_gain"], w["conv_w"], w["conv_b"],
      w["conv_ln_g"], w["conv_ln_b"], w["w_out"])


def _ffn_kernel(h_ref, fh_ref, n2g_ref, wg_ref, wu_ref, fcw_ref, fcb_ref, wd_ref,
                y_ref, fst_ref, hn_buf, g_buf, carry_buf, *, ns, seg, cr, carry):
    tm = ns * seg
    j = pl.program_id(1)
    c = pl.program_id(2)

    @pl.when(c == 0)
    def _():
        nr = min(seg, NORM_ROWS)
        for t0 in range(0, tm, nr):
            h = _rows(h_ref, t0, nr, seg)
            ms = jnp.mean(h * h, axis=-1, keepdims=True)
            hn_buf[t0:t0 + nr, :] = (h * lax.rsqrt(ms + EPS) * n2g_ref[...]).astype(_BF16)
            y_ref[t0 // seg, t0 % seg:t0 % seg + nr, :] = h

    if carry:
        @pl.when(j == 0)
        def _():
            carry_buf[c, FFN_HDR - 2:FFN_HDR, :] = fh_ref[0]

        g_buf[FFN_HDR - 2:FFN_HDR, :] = carry_buf[c, FFN_HDR - 2:FFN_HDR, :]
    else:
        g_buf[FFN_HDR - 2:FFN_HDR, :] = fh_ref[0]

    def project(r):
        t0 = r * cr
        hn = hn_buf[t0:t0 + cr, :]
        gp = _dot(hn, wg_ref[...])
        up = _dot(hn, wu_ref[...])
        g_buf[FFN_HDR + t0:FFN_HDR + t0 + cr, :] = gp
        return gp, up

    def finish(r, gp, up):
        t0 = r * cr
        s1 = g_buf[FFN_HDR + t0 - 1:FFN_HDR + t0 - 1 + cr, :]
        s2 = g_buf[FFN_HDR + t0 - 2:FFN_HDR + t0 - 2 + cr, :]
        if seg < cr:
            nsc = cr // seg
            fh = fh_ref[t0 // seg:t0 // seg + nsc]
            h0 = fh[:, 0:1, :]
            h1 = fh[:, 1:2, :]
            rid = lax.broadcasted_iota(jnp.int32, (nsc, seg, FF_BLOCK), 1)
            s1 = jnp.where(rid == 0, h1, s1.reshape(nsc, seg, FF_BLOCK)).reshape(cr, FF_BLOCK)
            s2 = jnp.where(rid == 0, h0, jnp.where(rid == 1, h1, s2.reshape(nsc, seg, FF_BLOCK))).reshape(cr, FF_BLOCK)
        gc = fcw_ref[0:1, :] * s2 + fcw_ref[1:2, :] * s1 + fcw_ref[2:3, :] * gp + fcb_ref[...]
        act = (gc * jax.nn.sigmoid(gc)) * up
        down = _dot(act.astype(_BF16), wd_ref[...])
        if seg >= cr:
            y_ref[t0 // seg, t0 % seg:t0 % seg + cr, :] += down
        else:
            y_ref[t0 // seg:(t0 + cr) // seg, :, :] += down.reshape(cr // seg, seg, D_MODEL)

    nchain = tm // cr
    nxt = project(0)
    for r in range(nchain):
        cur = nxt
        if r + 1 < nchain:
            nxt = project(r + 1)
        finish(r, *cur)

    last = g_buf[FFN_HDR:FFN_HDR + tm, :].reshape(ns, seg, FF_BLOCK)[:, seg - (FFN_CONV_K - 1):seg, :]
    fst_ref[...] = last[:, None]
    if carry:
        carry_buf[c, FFN_HDR - 2:FFN_HDR, :] = g_buf[FFN_HDR + tm - 2:FFN_HDR + tm, :]


def _ffn(h, fhist, w, *, ns, seg, cr, carry, hist_per_tile):
    n_seq, seq_len, _ = h.shape
    tm = ns * seg
    n_ff = D_FF // FF_BLOCK
    grid = (n_seq // ns, seq_len // seg, n_ff)
    fh_map = (lambda b, j, c: (b, 0, c)) if hist_per_tile else (lambda b, j, c: (0, 0, c))
    in_specs = [
        pl.BlockSpec((ns, seg, D_MODEL), lambda b, j, c: (b, j, 0)),
        pl.BlockSpec((ns, FFN_CONV_K - 1, FF_BLOCK), fh_map),
        pl.BlockSpec((1, D_MODEL), lambda b, j, c: (0, 0)),
        pl.BlockSpec((D_MODEL, FF_BLOCK), lambda b, j, c: (0, c)),
        pl.BlockSpec((D_MODEL, FF_BLOCK), lambda b, j, c: (0, c)),
        pl.BlockSpec((FFN_CONV_K, FF_BLOCK), lambda b, j, c: (0, c)),
        pl.BlockSpec((1, FF_BLOCK), lambda b, j, c: (0, c)),
        pl.BlockSpec((FF_BLOCK, D_MODEL), lambda b, j, c: (c, 0)),
    ]
    out_specs = [
        pl.BlockSpec((ns, seg, D_MODEL), lambda b, j, c: (b, j, 0)),
        pl.BlockSpec((ns, 1, FFN_CONV_K - 1, FF_BLOCK), lambda b, j, c: (b, j, 0, c)),
    ]
    out_shape = [
        jax.ShapeDtypeStruct((n_seq, seq_len, D_MODEL), _F32),
        jax.ShapeDtypeStruct((n_seq, seq_len // seg, FFN_CONV_K - 1, D_FF), _F32),
    ]
    scratch = [
        pltpu.VMEM((tm, D_MODEL), _BF16),
        pltpu.VMEM((FFN_HDR + tm, FF_BLOCK), _F32),
        pltpu.VMEM((n_ff, FFN_HDR, FF_BLOCK), _F32),
    ]
    kern = functools.partial(_ffn_kernel, ns=ns, seg=seg, cr=cr, carry=carry)
    return pl.pallas_call(
        kern,
        grid=grid,
        in_specs=in_specs,
        out_specs=out_specs,
        out_shape=out_shape,
        scratch_shapes=scratch,
        compiler_params=pltpu.CompilerParams(
            dimension_semantics=("arbitrary", "arbitrary", "arbitrary"), vmem_limit_bytes=VMEM_LIMIT),
        name=f"ffn_ns{ns}_seg{seg}",
    )(h, fhist, w["norm2_g"], w["w_gate"], w["w_up"], w["ffn_conv_w"], w["ffn_conv_b"], w["w_down"])


def _gate_rows_kernel(h_ref, n2g_ref, wg_ref, o_ref):
    h = h_ref[...]
    ms = jnp.mean(h * h, axis=-1, keepdims=True)
    hn = (h * lax.rsqrt(ms + EPS) * n2g_ref[...]).astype(_BF16)
    o_ref[...] = _dot(hn, wg_ref[...])


def _gate_rows(h, w):
    n = h.shape[0]
    return pl.pallas_call(
        _gate_rows_kernel,
        grid=(D_FF // FF_BLOCK,),
        in_specs=[
            pl.BlockSpec((n, D_MODEL), lambda c: (0, 0)),
            pl.BlockSpec((1, D_MODEL), lambda c: (0, 0)),
            pl.BlockSpec((D_MODEL, FF_BLOCK), lambda c: (0, c)),
        ],
        out_specs=pl.BlockSpec((n, FF_BLOCK), lambda c: (0, c)),
        out_shape=jax.ShapeDtypeStruct((n, D_FF), _F32),
        compiler_params=pltpu.CompilerParams(dimension_semantics=("arbitrary",)),
        name="gate_rows",
    )(h, w["norm2_g"], w["w_gate"])


def _rope_tables(pos, reps):
    inv_freq = ROPE_THETA ** (-jnp.arange(ROPE_HALF, dtype=_F32) / ROPE_HALF)
    ang = pos.astype(_F32)[:, None] * inv_freq[None, :]
    cos, sin = jnp.cos(ang), jnp.sin(ang)
    n = pos.shape[0]
    rest = HEAD_DIM - 2 * ROPE_HALF
    c = jnp.concatenate([cos, cos, jnp.ones((n, rest), _F32)], axis=1)
    sa = jnp.concatenate([jnp.zeros((n, ROPE_HALF), _F32), sin, jnp.zeros((n, rest), _F32)], axis=1)
    sb = jnp.concatenate([-sin, jnp.zeros((n, ROPE_HALF + rest), _F32)], axis=1)
    return tuple(jnp.tile(t, (reps, LANES // HEAD_DIM)) for t in (c, sa, sb))


def kernel(x_prompt, x_sample, cache_k, cache_v, state_conv, state_ffn_conv, meta_tokens, norm1_g, w_in,
           q_norm_g, k_norm_g, sinks, conv_w, conv_b, conv_ln_g, conv_ln_b, w_out, norm2_g, w_gate, w_up,
           ffn_conv_w, ffn_conv_b, w_down):
    assert w_in.shape[0] == 1, "single-layer problem"
    bsz, seq_len, _ = x_prompt.shape
    dbsz, dseq, _ = x_sample.shape
    lane_pair = LANES // HEAD_DIM
    w = {
        "norm1_g": norm1_g[0][None],
        "w_in": w_in[0].astype(_BF16),
        "q_gain": jnp.tile(q_norm_g[0], lane_pair)[None],
        "k_gain": jnp.tile(k_norm_g[0], lane_pair)[None],
        "conv_w": conv_w[0],
        "conv_b": conv_b[0][None],
        "conv_ln_g": conv_ln_g[0][None],
        "conv_ln_b": conv_ln_b[0][None],
        "w_out": w_out[0].astype(_BF16),
        "norm2_g": norm2_g[0][None],
        "w_gate": w_gate[0].astype(_BF16),
        "w_up": w_up[0].astype(_BF16),
        "ffn_conv_w": ffn_conv_w[0],
        "ffn_conv_b": ffn_conv_b[0][None],
        "w_down": w_down[0].astype(_BF16),
    }
    sink = sinks[0]

    zk = jnp.zeros((1, WINDOW, LANES), _F32)
    zu = jnp.zeros((1, CONV_HDR, CONV_WIDTH), _F32)
    h_m, k_m, v_m, u_m = _mixer(
        meta_tokens[None], zk, zk, zu, _rope_tables(jnp.arange(N_META, dtype=jnp.int32), 1), sink, w,
        ns=1, seg=N_META, qb=N_META, br=N_META, carry=False, ninv_first=KEY_SPAN - N_META, hist_per_tile=False)
    gate_m = _gate_rows(h_m[0], w)
    fhist_p = gate_m[N_META - (FFN_CONV_K - 1):][None]
    uhist_p = jnp.pad(u_m, ((0, 0), (CONV_HDR - (CONV_K - 1), 0), (0, 0)))

    rope_p = _rope_tables(N_META + jnp.arange(seq_len, dtype=jnp.int32), 1)
    h_p, k_p, v_p, u_p = _mixer(
        x_prompt, k_m, v_m, uhist_p, rope_p, sink, w,
        ns=1, seg=512, qb=2 * CHUNK, br=128, carry=True, ninv_first=WINDOW - N_META, hist_per_tile=False)
    y_p, f_p = _ffn(h_p, fhist_p, w, ns=1, seg=512, cr=256, carry=True, hist_per_tile=False)

    ns_s = 4
    rope_s = _rope_tables(N_META + PAST_LEN + jnp.arange(dseq, dtype=jnp.int32), ns_s)
    uhist_s = jnp.pad(state_conv[0], ((0, 0), (CONV_HDR - (CONV_K - 1), 0), (0, 0)))
    h_s, k_s, v_s, u_s = _mixer(
        x_sample, cache_k[0].reshape(dbsz, WINDOW, LANES), cache_v[0].reshape(dbsz, WINDOW, LANES), uhist_s,
        rope_s, sink, w,
        ns=ns_s, seg=dseq, qb=dseq, br=256, carry=False, ninv_first=KEY_SPAN - WINDOW - dseq, hist_per_tile=True)
    y_s, f_s = _ffn(h_s, state_ffn_conv[0], w, ns=8, seg=dseq, cr=256, carry=False, hist_per_tile=True)

    kv_shape_p = (1, bsz, WINDOW, N_KV_HEADS, HEAD_DIM)
    kv_shape_s = (1, dbsz, WINDOW, N_KV_HEADS, HEAD_DIM)
    return (y_p, y_s, k_p.reshape(kv_shape_p), v_p.reshape(kv_shape_p), u_p[None], f_p[None, :, -1],
            k_s.reshape(kv_shape_s), v_s.reshape(kv_shape_s), u_s[None], f_s[None, :, -1])
```

```python
import functools

import jax
import jax.numpy as jnp
from jax import lax
from jax.experimental import pallas as pl
from jax.experimental.pallas import tpu as pltpu

D_MODEL = 2048
N_META = 16
PAST_LEN = 4096
CHUNK = 64
HEAD_DIM = 64
N_HEADS = 16
N_KV_HEADS = 2
GROUP = N_HEADS // N_KV_HEADS
ATTN_WIDTH = N_HEADS * HEAD_DIM
CONV_WIDTH = D_MODEL - ATTN_WIDTH
KV_COLS = N_KV_HEADS * HEAD_DIM
WINDOW = 128
ROPE_HALF = 8
ROPE_THETA = 500000.0
CONV_K = 31
FFN_CONV_K = 3
D_FF = 5632
EPS = 1e-6
NEG_BIG = -1e30
IN_COLS = ATTN_WIDTH + 2 * KV_COLS + 2 * CONV_WIDTH

LANES = 128
SUBLANES = 8
KEY_SPAN = 256
CONV_HDR = 32
CONV_ROWS = 128
NORM_ROWS = 16
OUT_COLS = 256
FFN_HDR = 8
FF_BLOCK = 512
VMEM_LIMIT = 56 * 1024 * 1024

_F32 = jnp.float32
_BF16 = jnp.bfloat16


def _dot(a, b):
    return jnp.dot(a, b, preferred_element_type=_F32)


def _head_norm_rope(xs, gain, rc, rsa, rsb):
    lo = lax.broadcasted_iota(jnp.int32, xs.shape, 1) < HEAD_DIM
    sq = xs * xs
    ss_lo = jnp.sum(jnp.where(lo, sq, 0.0), axis=-1, keepdims=True)
    ss_hi = jnp.sum(jnp.where(lo, 0.0, sq), axis=-1, keepdims=True)
    inv = jnp.where(lo, lax.rsqrt(ss_lo * (1.0 / HEAD_DIM) + EPS), lax.rsqrt(ss_hi * (1.0 / HEAD_DIM) + EPS))
    xn = xs * inv * gain
    return xn * rc + pltpu.roll(xn, ROPE_HALF, 1) * rsa + pltpu.roll(xn, LANES - ROPE_HALF, 1) * rsb


def _rows(ref, t0, n, seg, cols=slice(None)):
    if seg >= n:
        return ref[t0 // seg, t0 % seg:t0 % seg + n, cols]
    v = ref[t0 // seg:(t0 + n) // seg, :, cols]
    return v.reshape(n, v.shape[-1])


def _mixer_kernel(x_ref, kh_ref, vh_ref, uh_ref, rc_ref, rsa_ref, rsb_ref, sink_ref,
                  n1g_ref, win_ref, qg_ref, kg_ref, cw_ref, cb_ref, lng_ref, lnb_ref,
                  h_ref, kst_ref, vst_ref, ust_ref,
                  xn_buf, kctx, vctx, kbf, vdup, uctx, qhm, cpre, acb,
                  *, ns, seg, qb, br, carry, ninv_first):
    tm = ns * seg
    hdr = KEY_SPAN - qb
    pn = min(seg, br)
    j = pl.program_id(1)
    lo_lane = lax.broadcasted_iota(jnp.int32, (1, LANES), 1) < HEAD_DIM

    def dup_v(v, g):
        r = pltpu.roll(v, HEAD_DIM, 1)
        return jnp.where(lo_lane, v, r) if g == 0 else jnp.where(lo_lane, r, v)

    def load_history():
        kh = kh_ref[...]
        vh = vh_ref[...]
        kctx[:, hdr - WINDOW:hdr, :] = kh
        vctx[:, hdr - WINDOW:hdr, :] = vh
        kbf[:, hdr - WINDOW:hdr, :] = kh.astype(_BF16)
        vh2 = vh.reshape(ns * WINDOW, LANES)
        for g in range(N_KV_HEADS):
            vdup[g, :, hdr - WINDOW:hdr, :] = dup_v(vh2, g).reshape(ns, WINDOW, LANES).astype(_BF16)
        uctx[:, 0:CONV_HDR, :] = uh_ref[...]

    if carry:
        pl.when(j == 0)(load_history)
    else:
        load_history()
    if hdr > WINDOW:
        kbf[:, 0:hdr - WINDOW, :] = jnp.zeros((ns, hdr - WINDOW, LANES), _BF16)
        vdup[:, :, 0:hdr - WINDOW, :] = jnp.zeros((N_KV_HEADS, ns, hdr - WINDOW, LANES), _BF16)

    rows = GROUP * qb
    glu0 = ATTN_WIDTH + 2 * KV_COLS

    def chain_steps(blk):
        t0 = blk * br
        tr = slice(t0, t0 + br)
        pieces = [((t0 + i * pn) // seg, (t0 + i * pn) % seg) for i in range(br // pn)]
        probs = {}

        def rope_rows():
            return rc_ref[tr, :], rsa_ref[tr, :], rsb_ref[tr, :]

        def norm():
            nr = min(pn, NORM_ROWS)
            for r0 in range(0, br, nr):
                x = _rows(x_ref, t0 + r0, nr, seg)
                ms = jnp.mean(x * x, axis=-1, keepdims=True)
                xn_buf[r0:r0 + nr, :] = (x * lax.rsqrt(ms + EPS) * n1g_ref[...]).astype(_BF16)

        def keys_values():
            kv = _dot(xn_buf[...], win_ref[:, ATTN_WIDTH:ATTN_WIDTH + 2 * KV_COLS])
            k_new = _head_norm_rope(kv[:, 0:KV_COLS], kg_ref[...], *rope_rows())
            v_new = kv[:, KV_COLS:2 * KV_COLS]
            k_bf = k_new.astype(_BF16)
            v_dup = [dup_v(v_new, g).astype(_BF16) for g in range(N_KV_HEADS)]
            for i, (s, r) in enumerate(pieces):
                pr = slice(i * pn, (i + 1) * pn)
                kctx[s, hdr + r:hdr + r + pn, :] = k_new[pr]
                vctx[s, hdr + r:hdr + r + pn, :] = v_new[pr]
                kbf[s, hdr + r:hdr + r + pn, :] = k_bf[pr]
                for g in range(N_KV_HEADS):
                    vdup[g, s, hdr + r:hdr + r + pn, :] = v_dup[g][pr]

        def queries(nb):
            qq = _dot(xn_buf[...], win_ref[:, 256 * nb:256 * nb + 256])
            for half in range(2):
                m = 2 * nb + half
                g = (2 * m) // GROUP
                qr = _head_norm_rope(qq[:, LANES * half:LANES * (half + 1)], qg_ref[...], *rope_rows())
                qr = qr * (HEAD_DIM ** -0.5)
                sw = pltpu.roll(qr, HEAD_DIM, 1)
                if g == 0:
                    even, odd = jnp.where(lo_lane, qr, 0.0), jnp.where(lo_lane, sw, 0.0)
                else:
                    even, odd = jnp.where(lo_lane, 0.0, sw), jnp.where(lo_lane, 0.0, qr)
                qhm[2 * m, tr, :] = even.astype(_BF16)
                qhm[2 * m + 1, tr, :] = odd.astype(_BF16)

        def glu(nb):
            ga = _dot(xn_buf[...], win_ref[:, glu0 + 256 * nb:glu0 + 256 * nb + 256])
            gb = _dot(xn_buf[...], win_ref[:, glu0 + CONV_WIDTH + 256 * nb:glu0 + CONV_WIDTH + 256 * nb + 256])
            u = ga * jax.nn.sigmoid(gb)
            for i, (s, r) in enumerate(pieces):
                uctx[s, CONV_HDR + r:CONV_HDR + r + pn, 256 * nb:256 * nb + 256] = u[i * pn:(i + 1) * pn]

        def scores(i, b, g):
            s, r = pieces[i]
            q0 = t0 + i * pn + b * qb
            k0 = r + b * qb
            if carry:
                ninv = jnp.where(j * (seg // qb) + k0 // qb == 0, ninv_first, 0)
            else:
                ninv = ninv_first
            col = lax.broadcasted_iota(jnp.int32, (1, KEY_SPAN), 1)
            if qb == 2 * CHUNK:
                first_chunk = (lax.broadcasted_iota(jnp.int32, (rows, 1), 0) & (qb - 1)) < CHUNK
                first_key = jnp.where(first_chunk, ninv, jnp.maximum(ninv, CHUNK))
                end_key = jnp.where(first_chunk, KEY_SPAN - CHUNK, KEY_SPAN)
                valid = (col >= first_key) & (col < end_key)
            else:
                valid = jnp.broadcast_to(col, (rows, KEY_SPAN)) >= ninv
            q = qhm[GROUP * g:GROUP * (g + 1), q0:q0 + qb, :].reshape(rows, LANES)
            kc = kbf[s, k0:k0 + KEY_SPAN, :]
            sc = lax.dot_general(q, kc, (((1,), (1,)), ((), ())), preferred_element_type=_F32)
            sc = jnp.where(valid, sc, NEG_BIG)
            sink = sink_ref[g]
            mx = jnp.maximum(jnp.max(sc, axis=-1, keepdims=True), sink)
            p = jnp.exp(sc - mx)
            denom = jnp.sum(p, axis=-1, keepdims=True) + jnp.exp(sink - mx)
            probs[i, b, g] = (p.astype(_BF16), denom)

        def values(i, b, g):
            s, r = pieces[i]
            q0 = t0 + i * pn + b * qb
            k0 = r + b * qb
            p, denom = probs[i, b, g]
            o = _dot(p, vdup[g, s, k0:k0 + KEY_SPAN, :]) / denom
            for m4 in range(GROUP // 2):
                slab = jnp.where(lo_lane, o[(2 * m4) * qb:(2 * m4 + 1) * qb], o[(2 * m4 + 1) * qb:(2 * m4 + 2) * qb])
                c0 = LANES * ((GROUP // 2) * g + m4)
                acb[blk, q0 - t0:q0 - t0 + qb, c0:c0 + LANES] = slab.astype(_BF16)

        def conv(cv):
            lanes = slice(LANES * cv, LANES * (cv + 1))
            cn = min(pn, CONV_ROWS)
            for i, (s, r0) in enumerate(pieces):
                for sub in range(pn // cn):
                    r = r0 + sub * cn
                    acc = jnp.broadcast_to(cb_ref[:, lanes], (cn, LANES))
                    for back in range(SUBLANES):
                        offs = [o for o in range(CONV_HDR - CONV_K + 1, CONV_HDR + 1) if (o + back) % SUBLANES == 0]
                        lead = SUBLANES if back else 0
                        part = None
                        for o in offs:
                            a8 = o + back
                            term = uctx[s, r + a8 - lead:r + a8 + cn, lanes] * cw_ref[o - 2:o - 1, lanes]
                            part = term if part is None else part + term
                        acc = acc + part[lead - back:lead - back + cn]
                    c0 = i * pn + sub * cn
                    cpre[c0:c0 + cn, lanes] = acc

        def layer_norm_silu():
            cp = cpre[...]
            mu = jnp.mean(cp, axis=-1, keepdims=True)
            xc = cp - mu
            var = jnp.mean(xc * xc, axis=-1, keepdims=True)
            yn = xc * lax.rsqrt(var + EPS) * lng_ref[...] + lnb_ref[...]
            acb[blk, :, ATTN_WIDTH:D_MODEL] = (yn * jax.nn.sigmoid(yn)).astype(_BF16)

        def project_out(nb):
            cols = slice(OUT_COLS * nb, OUT_COLS * (nb + 1))
            wcols = slice(IN_COLS + OUT_COLS * nb, IN_COLS + OUT_COLS * (nb + 1))
            hv = _rows(x_ref, t0, br, seg, cols) + _dot(acb[blk], win_ref[:, wcols])
            if seg >= br:
                h_ref[t0 // seg, t0 % seg:t0 % seg + br, cols] = hv
            else:
                h_ref[t0 // seg:(t0 + br) // seg, :, cols] = hv.reshape(br // seg, seg, OUT_COLS)

        blocks = [(i, b, g) for i in range(len(pieces)) for b in range(pn // qb) for g in range(N_KV_HEADS)]
        return dict(
            first=[norm, keys_values] + [functools.partial(queries, nb) for nb in range(ATTN_WIDTH // 256)],
            glu=[functools.partial(glu, nb) for nb in range(CONV_WIDTH // 256)],
            scores=[functools.partial(scores, *k) for k in blocks],
            values=[functools.partial(values, *k) for k in blocks],
            vector=[functools.partial(conv, cv) for cv in range(CONV_WIDTH // LANES)] + [layer_norm_silu],
            out=[functools.partial(project_out, nb) for nb in range(D_MODEL // OUT_COLS)])

    nblk = tm // br
    steps = {}
    for it in range(nblk + 2):
        matmul_steps, vector_steps = [], []
        if it < nblk:
            steps[it] = chain_steps(it)
        if 0 <= it - 1 < nblk:
            matmul_steps += steps[it - 1]["scores"]
            vector_steps = steps[it - 1]["vector"]
        if 0 <= it - 2 < nblk:
            matmul_steps += steps.pop(it - 2)["out"]
        if it < nblk:
            matmul_steps += steps[it]["first"]
        if 0 <= it - 1 < nblk:
            matmul_steps += steps[it - 1]["values"]
        if it < nblk:
            matmul_steps += steps[it]["glu"]
        for n in range(max(len(matmul_steps), len(vector_steps))):
            if n < len(matmul_steps):
                matmul_steps[n]()
            if n < len(vector_steps):
                vector_steps[n]()

    kst_ref[...] = kctx[:, hdr + seg - WINDOW:hdr + seg, :]
    vst_ref[...] = vctx[:, hdr + seg - WINDOW:hdr + seg, :]
    ust_ref[...] = uctx[:, seg + CONV_HDR - (CONV_K - 1):seg + CONV_HDR, :]
    if carry:
        kctx[:, 0:WINDOW, :] = kctx[:, seg:seg + WINDOW, :]
        vctx[:, 0:WINDOW, :] = vctx[:, seg:seg + WINDOW, :]
        kbf[:, 0:WINDOW, :] = kbf[:, seg:seg + WINDOW, :]
        vdup[:, :, 0:WINDOW, :] = vdup[:, :, seg:seg + WINDOW, :]
        uctx[:, 0:CONV_HDR, :] = uctx[:, seg:seg + CONV_HDR, :]


def _const_spec(shape, single_buffer=False):
    zeros = (0,) * len(shape)
    if single_buffer:
        return pl.BlockSpec(shape, lambda b, j: zeros, pipeline_mode=pl.Buffered(1))
    return pl.BlockSpec(shape, lambda b, j: zeros)


def _mixer(x, khist, vhist, uhist, rope, sinks, w, *, ns, seg, qb, br, carry, ninv_first, hist_per_tile):
    n_seq, seq_len, _ = x.shape
    tm = ns * seg
    hdr = KEY_SPAN - qb
    grid = (n_seq // ns, seq_len // seg)
    hist_map = (lambda b, j: (b, 0, 0)) if hist_per_tile else (lambda b, j: (0, 0, 0))
    sink_cols = jnp.repeat(sinks.reshape(N_KV_HEADS, GROUP), qb, axis=1)[..., None].astype(_F32)
    rc, rsa, rsb = rope
    in_specs = [
        pl.BlockSpec((ns, seg, D_MODEL), lambda b, j: (b, j, 0)),
        pl.BlockSpec((ns, WINDOW, LANES), hist_map),
        pl.BlockSpec((ns, WINDOW, LANES), hist_map),
        pl.BlockSpec((ns, CONV_HDR, CONV_WIDTH), hist_map),
        pl.BlockSpec((tm, LANES), lambda b, j: (j, 0)),
        pl.BlockSpec((tm, LANES), lambda b, j: (j, 0)),
        pl.BlockSpec((tm, LANES), lambda b, j: (j, 0)),
        _const_spec((N_KV_HEADS, GROUP * qb, 1), single_buffer=True),
        _const_spec((1, D_MODEL)),
        _const_spec((D_MODEL, IN_COLS + D_MODEL), single_buffer=True),
        _const_spec((1, LANES)),
        _const_spec((1, LANES)),
        _const_spec((CONV_K, CONV_WIDTH)),
        _const_spec((1, CONV_WIDTH)),
        _const_spec((1, CONV_WIDTH)),
        _const_spec((1, CONV_WIDTH)),
    ]
    out_specs = [
        pl.BlockSpec((ns, seg, D_MODEL), lambda b, j: (b, j, 0)),
        pl.BlockSpec((ns, WINDOW, LANES), lambda b, j: (b, 0, 0)),
        pl.BlockSpec((ns, WINDOW, LANES), lambda b, j: (b, 0, 0)),
        pl.BlockSpec((ns, CONV_K - 1, CONV_WIDTH), lambda b, j: (b, 0, 0)),
    ]
    out_shape = [
        jax.ShapeDtypeStruct((n_seq, seq_len, D_MODEL), _F32),
        jax.ShapeDtypeStruct((n_seq, WINDOW, LANES), _F32),
        jax.ShapeDtypeStruct((n_seq, WINDOW, LANES), _F32),
        jax.ShapeDtypeStruct((n_seq, CONV_K - 1, CONV_WIDTH), _F32),
    ]
    scratch = [
        pltpu.VMEM((br, D_MODEL), _BF16),
        pltpu.VMEM((ns, hdr + seg, LANES), _F32),
        pltpu.VMEM((ns, hdr + seg, LANES), _F32),
        pltpu.VMEM((ns, hdr + seg, LANES), _BF16),
        pltpu.VMEM((N_KV_HEADS, ns, hdr + seg, LANES), _BF16),
        pltpu.VMEM((ns, CONV_HDR + seg, CONV_WIDTH), _F32),
        pltpu.VMEM((N_HEADS, tm, LANES), _BF16),
        pltpu.VMEM((br, CONV_WIDTH), _F32),
        pltpu.VMEM((tm // br, br, D_MODEL), _BF16),
    ]
    kern = functools.partial(_mixer_kernel, ns=ns, seg=seg, qb=qb, br=br, carry=carry, ninv_first=ninv_first)
    return pl.pallas_call(
        kern,
        grid=grid,
        in_specs=in_specs,
        out_specs=out_specs,
        out_shape=out_shape,
        scratch_shapes=scratch,
        compiler_params=pltpu.CompilerParams(
            dimension_semantics=("arbitrary", "arbitrary"), vmem_limit_bytes=VMEM_LIMIT),
        name=f"mixer_ns{ns}_seg{seg}",
    )(x, khist, vhist, uhist, rc, rsa, rsb, sink_cols,
      w["norm1_g"], w["w_mix"], w["q_gain"], w["k_gain"], w["conv_w"], w["conv_b"],
      w["conv_ln_g"], w["conv_ln_b"])


def _ffn_kernel(h_ref, fh_ref, n2g_ref, wg_ref, wu_ref, fcw_ref, fcb_ref, wd_ref,
                y_ref, fst_ref, hn_buf, g_buf, carry_buf, *, ns, seg, cr, carry):
    tm = ns * seg
    j = pl.program_id(1)
    c = pl.program_id(2)

    @pl.when(c == 0)
    def _():
        nr = min(seg, NORM_ROWS)
        for t0 in range(0, tm, nr):
            h = _rows(h_ref, t0, nr, seg)
            ms = jnp.mean(h * h, axis=-1, keepdims=True)
            hn_buf[t0:t0 + nr, :] = (h * lax.rsqrt(ms + EPS) * n2g_ref[...]).astype(_BF16)
            y_ref[t0 // seg, t0 % seg:t0 % seg + nr, :] = h

    if carry:
        @pl.when(j == 0)
        def _():
            carry_buf[c, FFN_HDR - 2:FFN_HDR, :] = fh_ref[0]

        g_buf[FFN_HDR - 2:FFN_HDR, :] = carry_buf[c, FFN_HDR - 2:FFN_HDR, :]
    else:
        g_buf[FFN_HDR - 2:FFN_HDR, :] = fh_ref[0]

    def project(r):
        t0 = r * cr
        hn = hn_buf[t0:t0 + cr, :]
        gp = _dot(hn, wg_ref[...])
        up = _dot(hn, wu_ref[...])
        g_buf[FFN_HDR + t0:FFN_HDR + t0 + cr, :] = gp
        return gp, up

    def finish(r, gp, up):
        t0 = r * cr
        s1 = g_buf[FFN_HDR + t0 - 1:FFN_HDR + t0 - 1 + cr, :]
        s2 = g_buf[FFN_HDR + t0 - 2:FFN_HDR + t0 - 2 + cr, :]
        if seg < cr:
            nsc = cr // seg
            fh = fh_ref[t0 // seg:t0 // seg + nsc]
            h0 = fh[:, 0:1, :]
            h1 = fh[:, 1:2, :]
            rid = lax.broadcasted_iota(jnp.int32, (nsc, seg, FF_BLOCK), 1)
            s1 = jnp.where(rid == 0, h1, s1.reshape(nsc, seg, FF_BLOCK)).reshape(cr, FF_BLOCK)
            s2 = jnp.where(rid == 0, h0, jnp.where(rid == 1, h1, s2.reshape(nsc, seg, FF_BLOCK))).reshape(cr, FF_BLOCK)
        gc = fcw_ref[0:1, :] * s2 + fcw_ref[1:2, :] * s1 + fcw_ref[2:3, :] * gp + fcb_ref[...]
        act = (gc * jax.nn.sigmoid(gc)) * up
        down = _dot(act.astype(_BF16), wd_ref[...])
        if seg >= cr:
            y_ref[t0 // seg, t0 % seg:t0 % seg + cr, :] += down
        else:
            y_ref[t0 // seg:(t0 + cr) // seg, :, :] += down.reshape(cr // seg, seg, D_MODEL)

    nchain = tm // cr
    nxt = project(0)
    for r in range(nchain):
        cur = nxt
        if r + 1 < nchain:
            nxt = project(r + 1)
        finish(r, *cur)

    last = g_buf[FFN_HDR:FFN_HDR + tm, :].reshape(ns, seg, FF_BLOCK)[:, seg - (FFN_CONV_K - 1):seg, :]
    fst_ref[...] = last[:, None]
    if carry:
        carry_buf[c, FFN_HDR - 2:FFN_HDR, :] = g_buf[FFN_HDR + tm - 2:FFN_HDR + tm, :]


def _ffn(h, fhist, w, *, ns, seg, cr, carry, hist_per_tile):
    n_seq, seq_len, _ = h.shape
    tm = ns * seg
    n_ff = D_FF // FF_BLOCK
    grid = (n_seq // ns, seq_len // seg, n_ff)
    fh_map = (lambda b, j, c: (b, 0, c)) if hist_per_tile else (lambda b, j, c: (0, 0, c))
    in_specs = [
        pl.BlockSpec((ns, seg, D_MODEL), lambda b, j, c: (b, j, 0)),
        pl.BlockSpec((ns, FFN_CONV_K - 1, FF_BLOCK), fh_map),
        pl.BlockSpec((1, D_MODEL), lambda b, j, c: (0, 0)),
        pl.BlockSpec((D_MODEL, FF_BLOCK), lambda b, j, c: (0, c)),
        pl.BlockSpec((D_MODEL, FF_BLOCK), lambda b, j, c: (0, c)),
        pl.BlockSpec((FFN_CONV_K, FF_BLOCK), lambda b, j, c: (0, c)),
        pl.BlockSpec((1, FF_BLOCK), lambda b, j, c: (0, c)),
        pl.BlockSpec((FF_BLOCK, D_MODEL), lambda b, j, c: (c, 0)),
    ]
    out_specs = [
        pl.BlockSpec((ns, seg, D_MODEL), lambda b, j, c: (b, j, 0)),
        pl.BlockSpec((ns, 1, FFN_CONV_K - 1, FF_BLOCK), lambda b, j, c: (b, j, 0, c)),
    ]
    out_shape = [
        jax.ShapeDtypeStruct((n_seq, seq_len, D_MODEL), _F32),
        jax.ShapeDtypeStruct((n_seq, seq_len // seg, FFN_CONV_K - 1, D_FF), _F32),
    ]
    scratch = [
        pltpu.VMEM((tm, D_MODEL), _BF16),
        pltpu.VMEM((FFN_HDR + tm, FF_BLOCK), _F32),
        pltpu.VMEM((n_ff, FFN_HDR, FF_BLOCK), _F32),
    ]
    kern = functools.partial(_ffn_kernel, ns=ns, seg=seg, cr=cr, carry=carry)
    return pl.pallas_call(
        kern,
        grid=grid,
        in_specs=in_specs,
        out_specs=out_specs,
        out_shape=out_shape,
        scratch_shapes=scratch,
        compiler_params=pltpu.CompilerParams(
            dimension_semantics=("arbitrary", "arbitrary", "arbitrary"), vmem_limit_bytes=VMEM_LIMIT),
        name=f"ffn_ns{ns}_seg{seg}",
    )(h, fhist, w["norm2_g"], w["w_gate"], w["w_up"], w["ffn_conv_w"], w["ffn_conv_b"], w["w_down"])


def _gate_rows_kernel(h_ref, n2g_ref, wg_ref, o_ref):
    h = h_ref[...]
    ms = jnp.mean(h * h, axis=-1, keepdims=True)
    hn = (h * lax.rsqrt(ms + EPS) * n2g_ref[...]).astype(_BF16)
    o_ref[...] = _dot(hn, wg_ref[...])


def _gate_rows(h, w):
    n = h.shape[0]
    return pl.pallas_call(
        _gate_rows_kernel,
        grid=(D_FF // FF_BLOCK,),
        in_specs=[
            pl.BlockSpec((n, D_MODEL), lambda c: (0, 0)),
            pl.BlockSpec((1, D_MODEL), lambda c: (0, 0)),
            pl.BlockSpec((D_MODEL, FF_BLOCK), lambda c: (0, c)),
        ],
        out_specs=pl.BlockSpec((n, FF_BLOCK), lambda c: (0, c)),
        out_shape=jax.ShapeDtypeStruct((n, D_FF), _F32),
        compiler_params=pltpu.CompilerParams(dimension_semantics=("arbitrary",)),
        name="gate_rows",
    )(h, w["norm2_g"], w["w_gate"])


def _rope_tables(pos, reps):
    inv_freq = ROPE_THETA ** (-jnp.arange(ROPE_HALF, dtype=_F32) / ROPE_HALF)
    ang = pos.astype(_F32)[:, None] * inv_freq[None, :]
    cos, sin = jnp.cos(ang), jnp.sin(ang)
    n = pos.shape[0]
    rest = HEAD_DIM - 2 * ROPE_HALF
    c = jnp.concatenate([cos, cos, jnp.ones((n, rest), _F32)], axis=1)
    sa = jnp.concatenate([jnp.zeros((n, ROPE_HALF), _F32), sin, jnp.zeros((n, rest), _F32)], axis=1)
    sb = jnp.concatenate([-sin, jnp.zeros((n, ROPE_HALF + rest), _F32)], axis=1)
    return tuple(jnp.tile(t, (reps, LANES // HEAD_DIM)) for t in (c, sa, sb))


def kernel(x_prompt, x_sample, cache_k, cache_v, state_conv, state_ffn_conv, meta_tokens, norm1_g, w_in,
           q_norm_g, k_norm_g, sinks, conv_w, conv_b, conv_ln_g, conv_ln_b, w_out, norm2_g, w_gate, w_up,
           ffn_conv_w, ffn_conv_b, w_down):
    assert w_in.shape[0] == 1, "single-layer problem"
    bsz, seq_len, _ = x_prompt.shape
    dbsz, dseq, _ = x_sample.shape
    lane_pair = LANES // HEAD_DIM
    w = {
        "norm1_g": norm1_g[0][None],
        "w_mix": jnp.concatenate([w_in[0], w_out[0]], axis=1).astype(_BF16),
        "q_gain": jnp.tile(q_norm_g[0], lane_pair)[None],
        "k_gain": jnp.tile(k_norm_g[0], lane_pair)[None],
        "conv_w": conv_w[0],
        "conv_b": conv_b[0][None],
        "conv_ln_g": conv_ln_g[0][None],
        "conv_ln_b": conv_ln_b[0][None],
        "norm2_g": norm2_g[0][None],
        "w_gate": w_gate[0].astype(_BF16),
        "w_up": w_up[0].astype(_BF16),
        "ffn_conv_w": ffn_conv_w[0],
        "ffn_conv_b": ffn_conv_b[0][None],
        "w_down": w_down[0].astype(_BF16),
    }
    sink = sinks[0]

    zk = jnp.zeros((1, WINDOW, LANES), _F32)
    zu = jnp.zeros((1, CONV_HDR, CONV_WIDTH), _F32)
    h_m, k_m, v_m, u_m = _mixer(
        meta_tokens[None], zk, zk, zu, _rope_tables(jnp.arange(N_META, dtype=jnp.int32), 1), sink, w,
        ns=1, seg=N_META, qb=N_META, br=N_META, carry=False, ninv_first=KEY_SPAN - N_META, hist_per_tile=False)
    gate_m = _gate_rows(h_m[0], w)
    fhist_p = gate_m[N_META - (FFN_CONV_K - 1):][None]
    uhist_p = jnp.pad(u_m, ((0, 0), (CONV_HDR - (CONV_K - 1), 0), (0, 0)))

    rope_p = _rope_tables(N_META + jnp.arange(seq_len, dtype=jnp.int32), 1)
    h_p, k_p, v_p, u_p = _mixer(
        x_prompt, k_m, v_m, uhist_p, rope_p, sink, w,
        ns=1, seg=256, qb=2 * CHUNK, br=128, carry=True, ninv_first=WINDOW - N_META, hist_per_tile=False)
    y_p, f_p = _ffn(h_p, fhist_p, w, ns=1, seg=512, cr=256, carry=True, hist_per_tile=False)

    ns_s = 4
    rope_s = _rope_tables(N_META + PAST_LEN + jnp.arange(dseq, dtype=jnp.int32), ns_s)
    uhist_s = jnp.pad(state_conv[0], ((0, 0), (CONV_HDR - (CONV_K - 1), 0), (0, 0)))
    h_s, k_s, v_s, u_s = _mixer(
        x_sample, cache_k[0].reshape(dbsz, WINDOW, LANES), cache_v[0].reshape(dbsz, WINDOW, LANES), uhist_s,
        rope_s, sink, w,
        ns=ns_s, seg=dseq, qb=dseq, br=256, carry=False, ninv_first=KEY_SPAN - WINDOW - dseq, hist_per_tile=True)
    y_s, f_s = _ffn(h_s, state_ffn_conv[0], w, ns=8, seg=dseq, cr=256, carry=False, hist_per_tile=True)

    kv_shape_p = (1, bsz, WINDOW, N_KV_HEADS, HEAD_DIM)
    kv_shape_s = (1, dbsz, WINDOW, N_KV_HEADS, HEAD_DIM)
    return (y_p, y_s, k_p.reshape(kv_shape_p), v_p.reshape(kv_shape_p), u_p[None], f_p[None, :, -1],
            k_s.reshape(kv_shape_s), v_s.reshape(kv_shape_s), u_s[None], f_s[None, :, -1])
```

```python
import functools

import jax
import jax.numpy as jnp
from jax import lax
from jax.experimental import pallas as pl
from jax.experimental.pallas import tpu as pltpu

D_MODEL = 2048
N_META = 16
PAST_LEN = 4096
CHUNK = 64
HEAD_DIM = 64
N_HEADS = 16
N_KV_HEADS = 2
GROUP = N_HEADS // N_KV_HEADS
ATTN_WIDTH = N_HEADS * HEAD_DIM
CONV_WIDTH = D_MODEL - ATTN_WIDTH
KV_COLS = N_KV_HEADS * HEAD_DIM
WINDOW = 128
ROPE_HALF = 8
ROPE_THETA = 500000.0
CONV_K = 31
FFN_CONV_K = 3
D_FF = 5632
EPS = 1e-6
NEG_BIG = -1e30
IN_COLS = ATTN_WIDTH + 2 * KV_COLS + 2 * CONV_WIDTH

LANES = 128
SUBLANES = 8
KEY_SPAN = 256
CONV_HDR = 32
CONV_ROWS = 128
NORM_ROWS = 16
OUT_COLS = 256
FFN_HDR = 8
FF_BLOCK = 512
VMEM_LIMIT = 56 * 1024 * 1024

_F32 = jnp.float32
_BF16 = jnp.bfloat16


def _dot(a, b):
    return jnp.dot(a, b, preferred_element_type=_F32)


def _head_norm_rope(xs, gain, rc, rsa, rsb):
    lo = lax.broadcasted_iota(jnp.int32, xs.shape, 1) < HEAD_DIM
    sq = xs * xs
    ss_lo = jnp.sum(jnp.where(lo, sq, 0.0), axis=-1, keepdims=True)
    ss_hi = jnp.sum(jnp.where(lo, 0.0, sq), axis=-1, keepdims=True)
    inv = jnp.where(lo, lax.rsqrt(ss_lo * (1.0 / HEAD_DIM) + EPS), lax.rsqrt(ss_hi * (1.0 / HEAD_DIM) + EPS))
    xn = xs * inv * gain
    return xn * rc + pltpu.roll(xn, ROPE_HALF, 1) * rsa + pltpu.roll(xn, LANES - ROPE_HALF, 1) * rsb


def _rows(ref, t0, n, seg, cols=slice(None)):
    if seg >= n:
        return ref[t0 // seg, t0 % seg:t0 % seg + n, cols]
    v = ref[t0 // seg:(t0 + n) // seg, :, cols]
    return v.reshape(n, v.shape[-1])


def _mixer_kernel(x_ref, kh_ref, vh_ref, uh_ref, rc_ref, rsa_ref, rsb_ref, sink_ref,
                  n1g_ref, win_ref, qg_ref, kg_ref, cw_ref, cb_ref, lng_ref, lnb_ref,
                  h_ref, kst_ref, vst_ref, ust_ref,
                  xn_buf, kctx, vctx, kbf, vdup, uctx, qhm, cpre, acb,
                  *, ns, seg, qb, br, carry, ninv_first):
    tm = ns * seg
    hdr = KEY_SPAN - qb
    j = pl.program_id(1)
    lo_lane = lax.broadcasted_iota(jnp.int32, (1, LANES), 1) < HEAD_DIM

    def dup_v(v, g):
        r = pltpu.roll(v, HEAD_DIM, 1)
        return jnp.where(lo_lane, v, r) if g == 0 else jnp.where(lo_lane, r, v)

    def load_history():
        kh = kh_ref[...]
        vh = vh_ref[...]
        kctx[:, hdr - WINDOW:hdr, :] = kh
        vctx[:, hdr - WINDOW:hdr, :] = vh
        kbf[:, hdr - WINDOW:hdr, :] = kh.astype(_BF16)
        vh2 = vh.reshape(ns * WINDOW, LANES)
        for g in range(N_KV_HEADS):
            vdup[g, :, hdr - WINDOW:hdr, :] = dup_v(vh2, g).reshape(ns, WINDOW, LANES).astype(_BF16)
        uctx[:, 0:CONV_HDR, :] = uh_ref[...]

    if carry:
        pl.when(j == 0)(load_history)
    else:
        load_history()
    if hdr > WINDOW:
        kbf[:, 0:hdr - WINDOW, :] = jnp.zeros((ns, hdr - WINDOW, LANES), _BF16)
        vdup[:, :, 0:hdr - WINDOW, :] = jnp.zeros((N_KV_HEADS, ns, hdr - WINDOW, LANES), _BF16)

    rows = GROUP * qb
    glu0 = ATTN_WIDTH + 2 * KV_COLS

    def row_steps(t0, n):
        blk = t0 // br
        tr = slice(t0, t0 + n)
        pn = min(seg, n)
        pieces = [((t0 + i * pn) // seg, (t0 + i * pn) % seg) for i in range(n // pn)]
        probs = {}
        masks = {}

        def rope_rows():
            return rc_ref[tr, :], rsa_ref[tr, :], rsb_ref[tr, :]

        def norm():
            nr = min(pn, NORM_ROWS)
            for r0 in range(0, n, nr):
                x = _rows(x_ref, t0 + r0, nr, seg)
                ms = jnp.mean(x * x, axis=-1, keepdims=True)
                xn_buf[r0:r0 + nr, :] = (x * lax.rsqrt(ms + EPS) * n1g_ref[...]).astype(_BF16)

        def keys_values():
            kv = _dot(xn_buf[...], win_ref[:, ATTN_WIDTH:ATTN_WIDTH + 2 * KV_COLS])
            k_new = _head_norm_rope(kv[:, 0:KV_COLS], kg_ref[...], *rope_rows())
            v_new = kv[:, KV_COLS:2 * KV_COLS]
            k_bf = k_new.astype(_BF16)
            v_dup = [dup_v(v_new, g).astype(_BF16) for g in range(N_KV_HEADS)]
            for i, (s, r) in enumerate(pieces):
                pr = slice(i * pn, (i + 1) * pn)
                kctx[s, hdr + r:hdr + r + pn, :] = k_new[pr]
                vctx[s, hdr + r:hdr + r + pn, :] = v_new[pr]
                kbf[s, hdr + r:hdr + r + pn, :] = k_bf[pr]
                for g in range(N_KV_HEADS):
                    vdup[g, s, hdr + r:hdr + r + pn, :] = v_dup[g][pr]

        def queries(nb):
            qq = _dot(xn_buf[...], win_ref[:, 256 * nb:256 * nb + 256])
            for half in range(2):
                m = 2 * nb + half
                g = (2 * m) // GROUP
                qr = _head_norm_rope(qq[:, LANES * half:LANES * (half + 1)], qg_ref[...], *rope_rows())
                qr = qr * (HEAD_DIM ** -0.5)
                sw = pltpu.roll(qr, HEAD_DIM, 1)
                if g == 0:
                    even, odd = jnp.where(lo_lane, qr, 0.0), jnp.where(lo_lane, sw, 0.0)
                else:
                    even, odd = jnp.where(lo_lane, 0.0, sw), jnp.where(lo_lane, 0.0, qr)
                qhm[2 * m, tr, :] = even.astype(_BF16)
                qhm[2 * m + 1, tr, :] = odd.astype(_BF16)

        def glu(nb):
            ga = _dot(xn_buf[...], win_ref[:, glu0 + 256 * nb:glu0 + 256 * nb + 256])
            gb = _dot(xn_buf[...], win_ref[:, glu0 + CONV_WIDTH + 256 * nb:glu0 + CONV_WIDTH + 256 * nb + 256])
            u = ga * jax.nn.sigmoid(gb)
            for i, (s, r) in enumerate(pieces):
                uctx[s, CONV_HDR + r:CONV_HDR + r + pn, 256 * nb:256 * nb + 256] = u[i * pn:(i + 1) * pn]

        def key_mask(i, b):
            if (i, b) in masks:
                return masks[i, b]
            k0 = pieces[i][1] + b * qb
            if carry:
                ninv = jnp.where(j * (seg // qb) + k0 // qb == 0, ninv_first, 0)
            else:
                ninv = ninv_first
            col = lax.broadcasted_iota(jnp.int32, (1, KEY_SPAN), 1)
            if qb == 2 * CHUNK:
                first_chunk = (lax.broadcasted_iota(jnp.int32, (rows, 1), 0) & (qb - 1)) < CHUNK
                first_key = jnp.where(first_chunk, ninv, jnp.maximum(ninv, CHUNK))
                end_key = jnp.where(first_chunk, KEY_SPAN - CHUNK, KEY_SPAN)
                masks[i, b] = (col >= first_key) & (col < end_key)
            else:
                masks[i, b] = jnp.broadcast_to(col, (rows, KEY_SPAN)) >= ninv
            return masks[i, b]

        def scores(i, b, g):
            s, r = pieces[i]
            q0 = t0 + i * pn + b * qb
            k0 = r + b * qb
            valid = key_mask(i, b)
            q = qhm[GROUP * g:GROUP * (g + 1), q0:q0 + qb, :].reshape(rows, LANES)
            kc = kbf[s, k0:k0 + KEY_SPAN, :]
            sc = lax.dot_general(q, kc, (((1,), (1,)), ((), ())), preferred_element_type=_F32)
            sc = jnp.where(valid, sc, NEG_BIG)
            sink = sink_ref[g]
            mx = jnp.maximum(jnp.max(sc, axis=-1, keepdims=True), sink)
            p = jnp.exp(sc - mx)
            denom = jnp.sum(p, axis=-1, keepdims=True) + jnp.exp(sink - mx)
            probs[i, b, g] = (p.astype(_BF16), denom)

        def values(i, b, g):
            s, r = pieces[i]
            q0 = t0 + i * pn + b * qb
            k0 = r + b * qb
            p, denom = probs[i, b, g]
            o = _dot(p, vdup[g, s, k0:k0 + KEY_SPAN, :]) / denom
            for m4 in range(GROUP // 2):
                slab = jnp.where(lo_lane, o[(2 * m4) * qb:(2 * m4 + 1) * qb], o[(2 * m4 + 1) * qb:(2 * m4 + 2) * qb])
                c0 = LANES * ((GROUP // 2) * g + m4)
                acb[blk, q0 - t0:q0 - t0 + qb, c0:c0 + LANES] = slab.astype(_BF16)

        def conv(cv):
            lanes = slice(LANES * cv, LANES * (cv + 1))
            cn = min(pn, CONV_ROWS)
            for i, (s, r0) in enumerate(pieces):
                for sub in range(pn // cn):
                    r = r0 + sub * cn
                    acc = jnp.broadcast_to(cb_ref[:, lanes], (cn, LANES))
                    for back in range(SUBLANES):
                        offs = [o for o in range(CONV_HDR - CONV_K + 1, CONV_HDR + 1) if (o + back) % SUBLANES == 0]
                        lead = SUBLANES if back else 0
                        part = None
                        for o in offs:
                            a8 = o + back
                            term = uctx[s, r + a8 - lead:r + a8 + cn, lanes] * cw_ref[o - 2:o - 1, lanes]
                            part = term if part is None else part + term
                        acc = acc + part[lead - back:lead - back + cn]
                    c0 = i * pn + sub * cn
                    cpre[c0:c0 + cn, lanes] = acc

        def layer_norm_silu():
            cp = cpre[...]
            mu = jnp.mean(cp, axis=-1, keepdims=True)
            xc = cp - mu
            var = jnp.mean(xc * xc, axis=-1, keepdims=True)
            yn = xc * lax.rsqrt(var + EPS) * lng_ref[...] + lnb_ref[...]
            acb[blk, :, ATTN_WIDTH:D_MODEL] = (yn * jax.nn.sigmoid(yn)).astype(_BF16)

        def project_out(nb):
            cols = slice(OUT_COLS * nb, OUT_COLS * (nb + 1))
            wcols = slice(IN_COLS + OUT_COLS * nb, IN_COLS + OUT_COLS * (nb + 1))
            hv = _rows(x_ref, t0, n, seg, cols) + _dot(acb[blk], win_ref[:, wcols])
            if seg >= n:
                h_ref[t0 // seg, t0 % seg:t0 % seg + n, cols] = hv
            else:
                h_ref[t0 // seg:(t0 + n) // seg, :, cols] = hv.reshape(n // seg, seg, OUT_COLS)

        blocks = [(i, b, g) for i in range(len(pieces)) for b in range(pn // qb) for g in range(N_KV_HEADS)]
        return dict(
            first=[norm, keys_values] + [functools.partial(queries, nb) for nb in range(ATTN_WIDTH // 256)],
            glu=[functools.partial(glu, nb) for nb in range(CONV_WIDTH // 256)],
            scores=[functools.partial(scores, *k) for k in blocks],
            values=[functools.partial(values, *k) for k in blocks],
            vector=[functools.partial(conv, cv) for cv in range(CONV_WIDTH // LANES)] + [layer_norm_silu],
            out=[functools.partial(project_out, nb) for nb in range(D_MODEL // OUT_COLS)])

    nblk = tm // br
    proj = row_steps(0, tm)
    chains = [row_steps(blk * br, br) for blk in range(nblk)]
    order = proj["first"] + chains[0]["scores"] + proj["glu"] + chains[0]["values"]
    for blk in range(nblk):
        nxt = chains[blk + 1] if blk + 1 < nblk else None
        matmul_steps = (nxt["scores"] + nxt["values"] if nxt else []) + (chains[blk - 1]["out"] if blk else [])
        vector_steps = chains[blk]["vector"]
        for n in range(max(len(matmul_steps), len(vector_steps))):
            if n < len(vector_steps):
                order.append(vector_steps[n])
            if n < len(matmul_steps):
                order.append(matmul_steps[n])
    order += chains[-1]["out"]
    for step in order:
        step()

    kst_ref[...] = kctx[:, hdr + seg - WINDOW:hdr + seg, :]
    vst_ref[...] = vctx[:, hdr + seg - WINDOW:hdr + seg, :]
    ust_ref[...] = uctx[:, seg + CONV_HDR - (CONV_K - 1):seg + CONV_HDR, :]
    if carry:
        kctx[:, 0:WINDOW, :] = kctx[:, seg:seg + WINDOW, :]
        vctx[:, 0:WINDOW, :] = vctx[:, seg:seg + WINDOW, :]
        kbf[:, 0:WINDOW, :] = kbf[:, seg:seg + WINDOW, :]
        vdup[:, :, 0:WINDOW, :] = vdup[:, :, seg:seg + WINDOW, :]
        uctx[:, 0:CONV_HDR, :] = uctx[:, seg:seg + CONV_HDR, :]


def _const_spec(shape, single_buffer=False):
    zeros = (0,) * len(shape)
    if single_buffer:
        return pl.BlockSpec(shape, lambda b, j: zeros, pipeline_mode=pl.Buffered(1))
    return pl.BlockSpec(shape, lambda b, j: zeros)


def _mixer(x, khist, vhist, uhist, rope, sinks, w, *, ns, seg, qb, br, carry, ninv_first, hist_per_tile):
    n_seq, seq_len, _ = x.shape
    tm = ns * seg
    hdr = KEY_SPAN - qb
    grid = (n_seq // ns, seq_len // seg)
    hist_map = (lambda b, j: (b, 0, 0)) if hist_per_tile else (lambda b, j: (0, 0, 0))
    sink_cols = jnp.repeat(sinks.reshape(N_KV_HEADS, GROUP), qb, axis=1)[..., None].astype(_F32)
    rc, rsa, rsb = rope
    in_specs = [
        pl.BlockSpec((ns, seg, D_MODEL), lambda b, j: (b, j, 0)),
        pl.BlockSpec((ns, WINDOW, LANES), hist_map),
        pl.BlockSpec((ns, WINDOW, LANES), hist_map),
        pl.BlockSpec((ns, CONV_HDR, CONV_WIDTH), hist_map),
        pl.BlockSpec((tm, LANES), lambda b, j: (j, 0)),
        pl.BlockSpec((tm, LANES), lambda b, j: (j, 0)),
        pl.BlockSpec((tm, LANES), lambda b, j: (j, 0)),
        _const_spec((N_KV_HEADS, GROUP * qb, 1), single_buffer=True),
        _const_spec((1, D_MODEL)),
        _const_spec((D_MODEL, IN_COLS + D_MODEL), single_buffer=True),
        _const_spec((1, LANES)),
        _const_spec((1, LANES)),
        _const_spec((CONV_K, CONV_WIDTH)),
        _const_spec((1, CONV_WIDTH)),
        _const_spec((1, CONV_WIDTH)),
        _const_spec((1, CONV_WIDTH)),
    ]
    out_specs = [
        pl.BlockSpec((ns, seg, D_MODEL), lambda b, j: (b, j, 0)),
        pl.BlockSpec((ns, WINDOW, LANES), lambda b, j: (b, 0, 0)),
        pl.BlockSpec((ns, WINDOW, LANES), lambda b, j: (b, 0, 0)),
        pl.BlockSpec((ns, CONV_K - 1, CONV_WIDTH), lambda b, j: (b, 0, 0)),
    ]
    out_shape = [
        jax.ShapeDtypeStruct((n_seq, seq_len, D_MODEL), _F32),
        jax.ShapeDtypeStruct((n_seq, WINDOW, LANES), _F32),
        jax.ShapeDtypeStruct((n_seq, WINDOW, LANES), _F32),
        jax.ShapeDtypeStruct((n_seq, CONV_K - 1, CONV_WIDTH), _F32),
    ]
    scratch = [
        pltpu.VMEM((tm, D_MODEL), _BF16),
        pltpu.VMEM((ns, hdr + seg, LANES), _F32),
        pltpu.VMEM((ns, hdr + seg, LANES), _F32),
        pltpu.VMEM((ns, hdr + seg, LANES), _BF16),
        pltpu.VMEM((N_KV_HEADS, ns, hdr + seg, LANES), _BF16),
        pltpu.VMEM((ns, CONV_HDR + seg, CONV_WIDTH), _F32),
        pltpu.VMEM((N_HEADS, tm, LANES), _BF16),
        pltpu.VMEM((br, CONV_WIDTH), _F32),
        pltpu.VMEM((tm // br, br, D_MODEL), _BF16),
    ]
    kern = functools.partial(_mixer_kernel, ns=ns, seg=seg, qb=qb, br=br, carry=carry, ninv_first=ninv_first)
    return pl.pallas_call(
        kern,
        grid=grid,
        in_specs=in_specs,
        out_specs=out_specs,
        out_shape=out_shape,
        scratch_shapes=scratch,
        compiler_params=pltpu.CompilerParams(
            dimension_semantics=("arbitrary", "arbitrary"), vmem_limit_bytes=VMEM_LIMIT),
        name=f"mixer_ns{ns}_seg{seg}",
    )(x, khist, vhist, uhist, rc, rsa, rsb, sink_cols,
      w["norm1_g"], w["w_mix"], w["q_gain"], w["k_gain"], w["conv_w"], w["conv_b"],
      w["conv_ln_g"], w["conv_ln_b"])


def _ffn_kernel(h_ref, fh_ref, n2g_ref, wg_ref, wu_ref, fcw_ref, fcb_ref, wd_ref,
                y_ref, fst_ref, hn_buf, g_buf, carry_buf, *, ns, seg, cr, carry):
    tm = ns * seg
    j = pl.program_id(1)
    c = pl.program_id(2)

    @pl.when(c == 0)
    def _():
        nr = min(seg, NORM_ROWS)
        for t0 in range(0, tm, nr):
            h = _rows(h_ref, t0, nr, seg)
            ms = jnp.mean(h * h, axis=-1, keepdims=True)
            hn_buf[t0:t0 + nr, :] = (h * lax.rsqrt(ms + EPS) * n2g_ref[...]).astype(_BF16)
            y_ref[t0 // seg, t0 % seg:t0 % seg + nr, :] = h

    if carry:
        @pl.when(j == 0)
        def _():
            carry_buf[c, FFN_HDR - 2:FFN_HDR, :] = fh_ref[0]

        g_buf[FFN_HDR - 2:FFN_HDR, :] = carry_buf[c, FFN_HDR - 2:FFN_HDR, :]
    else:
        g_buf[FFN_HDR - 2:FFN_HDR, :] = fh_ref[0]

    def project(r):
        t0 = r * cr
        hn = hn_buf[t0:t0 + cr, :]
        gp = _dot(hn, wg_ref[...])
        up = _dot(hn, wu_ref[...])
        g_buf[FFN_HDR + t0:FFN_HDR + t0 + cr, :] = gp
        return gp, up

    def finish(r, gp, up):
        t0 = r * cr
        s1 = g_buf[FFN_HDR + t0 - 1:FFN_HDR + t0 - 1 + cr, :]
        s2 = g_buf[FFN_HDR + t0 - 2:FFN_HDR + t0 - 2 + cr, :]
        if seg < cr:
            nsc = cr // seg
            fh = fh_ref[t0 // seg:t0 // seg + nsc]
            h0 = fh[:, 0:1, :]
            h1 = fh[:, 1:2, :]
            rid = lax.broadcasted_iota(jnp.int32, (nsc, seg, FF_BLOCK), 1)
            s1 = jnp.where(rid == 0, h1, s1.reshape(nsc, seg, FF_BLOCK)).reshape(cr, FF_BLOCK)
            s2 = jnp.where(rid == 0, h0, jnp.where(rid == 1, h1, s2.reshape(nsc, seg, FF_BLOCK))).reshape(cr, FF_BLOCK)
        gc = fcw_ref[0:1, :] * s2 + fcw_ref[1:2, :] * s1 + fcw_ref[2:3, :] * gp + fcb_ref[...]
        act = (gc * jax.nn.sigmoid(gc)) * up
        down = _dot(act.astype(_BF16), wd_ref[...])
        if seg >= cr:
            y_ref[t0 // seg, t0 % seg:t0 % seg + cr, :] += down
        else:
            y_ref[t0 // seg:(t0 + cr) // seg, :, :] += down.reshape(cr // seg, seg, D_MODEL)

    nchain = tm // cr
    nxt = project(0)
    for r in range(nchain):
        cur = nxt
        if r + 1 < nchain:
            nxt = project(r + 1)
        finish(r, *cur)

    last = g_buf[FFN_HDR:FFN_HDR + tm, :].reshape(ns, seg, FF_BLOCK)[:, seg - (FFN_CONV_K - 1):seg, :]
    fst_ref[...] = last[:, None]
    if carry:
        carry_buf[c, FFN_HDR - 2:FFN_HDR, :] = g_buf[FFN_HDR + tm - 2:FFN_HDR + tm, :]


def _ffn(h, fhist, w, *, ns, seg, cr, carry, hist_per_tile):
    n_seq, seq_len, _ = h.shape
    tm = ns * seg
    n_ff = D_FF // FF_BLOCK
    grid = (n_seq // ns, seq_len // seg, n_ff)
    fh_map = (lambda b, j, c: (b, 0, c)) if hist_per_tile else (lambda b, j, c: (0, 0, c))
    in_specs = [
        pl.BlockSpec((ns, seg, D_MODEL), lambda b, j, c: (b, j, 0)),
        pl.BlockSpec((ns, FFN_CONV_K - 1, FF_BLOCK), fh_map),
        pl.BlockSpec((1, D_MODEL), lambda b, j, c: (0, 0)),
        pl.BlockSpec((D_MODEL, FF_BLOCK), lambda b, j, c: (0, c)),
        pl.BlockSpec((D_MODEL, FF_BLOCK), lambda b, j, c: (0, c)),
        pl.BlockSpec((FFN_CONV_K, FF_BLOCK), lambda b, j, c: (0, c)),
        pl.BlockSpec((1, FF_BLOCK), lambda b, j, c: (0, c)),
        pl.BlockSpec((FF_BLOCK, D_MODEL), lambda b, j, c: (c, 0)),
    ]
    out_specs = [
        pl.BlockSpec((ns, seg, D_MODEL), lambda b, j, c: (b, j, 0)),
        pl.BlockSpec((ns, 1, FFN_CONV_K - 1, FF_BLOCK), lambda b, j, c: (b, j, 0, c)),
    ]
    out_shape = [
        jax.ShapeDtypeStruct((n_seq, seq_len, D_MODEL), _F32),
        jax.ShapeDtypeStruct((n_seq, seq_len // seg, FFN_CONV_K - 1, D_FF), _F32),
    ]
    scratch = [
        pltpu.VMEM((tm, D_MODEL), _BF16),
        pltpu.VMEM((FFN_HDR + tm, FF_BLOCK), _F32),
        pltpu.VMEM((n_ff, FFN_HDR, FF_BLOCK), _F32),
    ]
    kern = functools.partial(_ffn_kernel, ns=ns, seg=seg, cr=cr, carry=carry)
    return pl.pallas_call(
        kern,
        grid=grid,
        in_specs=in_specs,
        out_specs=out_specs,
        out_shape=out_shape,
        scratch_shapes=scratch,
        compiler_params=pltpu.CompilerParams(
            dimension_semantics=("arbitrary", "arbitrary", "arbitrary"), vmem_limit_bytes=VMEM_LIMIT),
        name=f"ffn_ns{ns}_seg{seg}",
    )(h, fhist, w["norm2_g"], w["w_gate"], w["w_up"], w["ffn_conv_w"], w["ffn_conv_b"], w["w_down"])


def _gate_rows_kernel(h_ref, n2g_ref, wg_ref, o_ref):
    h = h_ref[...]
    ms = jnp.mean(h * h, axis=-1, keepdims=True)
    hn = (h * lax.rsqrt(ms + EPS) * n2g_ref[...]).astype(_BF16)
    o_ref[...] = _dot(hn, wg_ref[...])


def _gate_rows(h, w):
    n = h.shape[0]
    return pl.pallas_call(
        _gate_rows_kernel,
        grid=(D_FF // FF_BLOCK,),
        in_specs=[
            pl.BlockSpec((n, D_MODEL), lambda c: (0, 0)),
            pl.BlockSpec((1, D_MODEL), lambda c: (0, 0)),
            pl.BlockSpec((D_MODEL, FF_BLOCK), lambda c: (0, c)),
        ],
        out_specs=pl.BlockSpec((n, FF_BLOCK), lambda c: (0, c)),
        out_shape=jax.ShapeDtypeStruct((n, D_FF), _F32),
        compiler_params=pltpu.CompilerParams(dimension_semantics=("arbitrary",)),
        name="gate_rows",
    )(h, w["norm2_g"], w["w_gate"])


def _rope_tables(pos, reps):
    inv_freq = ROPE_THETA ** (-jnp.arange(ROPE_HALF, dtype=_F32) / ROPE_HALF)
    ang = pos.astype(_F32)[:, None] * inv_freq[None, :]
    cos, sin = jnp.cos(ang), jnp.sin(ang)
    n = pos.shape[0]
    rest = HEAD_DIM - 2 * ROPE_HALF
    c = jnp.concatenate([cos, cos, jnp.ones((n, rest), _F32)], axis=1)
    sa = jnp.concatenate([jnp.zeros((n, ROPE_HALF), _F32), sin, jnp.zeros((n, rest), _F32)], axis=1)
    sb = jnp.concatenate([-sin, jnp.zeros((n, ROPE_HALF + rest), _F32)], axis=1)
    return tuple(jnp.tile(t, (reps, LANES // HEAD_DIM)) for t in (c, sa, sb))


def kernel(x_prompt, x_sample, cache_k, cache_v, state_conv, state_ffn_conv, meta_tokens, norm1_g, w_in,
           q_norm_g, k_norm_g, sinks, conv_w, conv_b, conv_ln_g, conv_ln_b, w_out, norm2_g, w_gate, w_up,
           ffn_conv_w, ffn_conv_b, w_down):
    assert w_in.shape[0] == 1, "single-layer problem"
    bsz, seq_len, _ = x_prompt.shape
    dbsz, dseq, _ = x_sample.shape
    lane_pair = LANES // HEAD_DIM
    w = {
        "norm1_g": norm1_g[0][None],
        "w_mix": jnp.concatenate([w_in[0], w_out[0]], axis=1).astype(_BF16),
        "q_gain": jnp.tile(q_norm_g[0], lane_pair)[None],
        "k_gain": jnp.tile(k_norm_g[0], lane_pair)[None],
        "conv_w": conv_w[0],
        "conv_b": conv_b[0][None],
        "conv_ln_g": conv_ln_g[0][None],
        "conv_ln_b": conv_ln_b[0][None],
        "norm2_g": norm2_g[0][None],
        "w_gate": w_gate[0].astype(_BF16),
        "w_up": w_up[0].astype(_BF16),
        "ffn_conv_w": ffn_conv_w[0],
        "ffn_conv_b": ffn_conv_b[0][None],
        "w_down": w_down[0].astype(_BF16),
    }
    sink = sinks[0]

    zk = jnp.zeros((1, WINDOW, LANES), _F32)
    zu = jnp.zeros((1, CONV_HDR, CONV_WIDTH), _F32)
    h_m, k_m, v_m, u_m = _mixer(
        meta_tokens[None], zk, zk, zu, _rope_tables(jnp.arange(N_META, dtype=jnp.int32), 1), sink, w,
        ns=1, seg=N_META, qb=N_META, br=N_META, carry=False, ninv_first=KEY_SPAN - N_META, hist_per_tile=False)
    gate_m = _gate_rows(h_m[0], w)
    fhist_p = gate_m[N_META - (FFN_CONV_K - 1):][None]
    uhist_p = jnp.pad(u_m, ((0, 0), (CONV_HDR - (CONV_K - 1), 0), (0, 0)))

    rope_p = _rope_tables(N_META + jnp.arange(seq_len, dtype=jnp.int32), 1)
    h_p, k_p, v_p, u_p = _mixer(
        x_prompt, k_m, v_m, uhist_p, rope_p, sink, w,
        ns=1, seg=256, qb=2 * CHUNK, br=128, carry=True, ninv_first=WINDOW - N_META, hist_per_tile=False)
    y_p, f_p = _ffn(h_p, fhist_p, w, ns=1, seg=512, cr=256, carry=True, hist_per_tile=False)

    ns_s = 4
    rope_s = _rope_tables(N_META + PAST_LEN + jnp.arange(dseq, dtype=jnp.int32), ns_s)
    uhist_s = jnp.pad(state_conv[0], ((0, 0), (CONV_HDR - (CONV_K - 1), 0), (0, 0)))
    h_s, k_s, v_s, u_s = _mixer(
        x_sample, cache_k[0].reshape(dbsz, WINDOW, LANES), cache_v[0].reshape(dbsz, WINDOW, LANES), uhist_s,
        rope_s, sink, w,
        ns=ns_s, seg=dseq, qb=dseq, br=256, carry=False, ninv_first=KEY_SPAN - WINDOW - dseq, hist_per_tile=True)
    y_s, f_s = _ffn(h_s, state_ffn_conv[0], w, ns=8, seg=dseq, cr=256, carry=False, hist_per_tile=True)

    kv_shape_p = (1, bsz, WINDOW, N_KV_HEADS, HEAD_DIM)
    kv_shape_s = (1, dbsz, WINDOW, N_KV_HEADS, HEAD_DIM)
    return (y_p, y_s, k_p.reshape(kv_shape_p), v_p.reshape(kv_shape_p), u_p[None], f_p[None, :, -1],
            k_s.reshape(kv_shape_s), v_s.reshape(kv_shape_s), u_s[None], f_s[None, :, -1])
```

```python
import functools

import jax
import jax.numpy as jnp
from jax import lax
from jax.experimental import pallas as pl
from jax.experimental.pallas import tpu as pltpu

D_MODEL = 2048
N_META = 16
PAST_LEN = 4096
CHUNK = 64
HEAD_DIM = 64
N_HEADS = 16
N_KV_HEADS = 2
GROUP = N_HEADS // N_KV_HEADS
ATTN_WIDTH = N_HEADS * HEAD_DIM
CONV_WIDTH = D_MODEL - ATTN_WIDTH
KV_COLS = N_KV_HEADS * HEAD_DIM
WINDOW = 128
ROPE_HALF = 8
ROPE_THETA = 500000.0
CONV_K = 31
FFN_CONV_K = 3
D_FF = 5632
EPS = 1e-6
NEG_BIG = -1e30
IN_COLS = ATTN_WIDTH + 2 * KV_COLS + 2 * CONV_WIDTH

LANES = 128
SUBLANES = 8
KEY_SPAN = 256
CONV_HDR = 32
CONV_ROWS = 128
NORM_ROWS = 16
OUT_COLS = 256
FFN_HDR = 8
FF_BLOCK = 512
VMEM_LIMIT = 60 * 1024 * 1024

_F32 = jnp.float32
_BF16 = jnp.bfloat16


def _dot(a, b):
    return jnp.dot(a, b, preferred_element_type=_F32)


def _head_norm_rope(xs, gain, rc, rsa, rsb):
    lo = lax.broadcasted_iota(jnp.int32, xs.shape, 1) < HEAD_DIM
    sq = xs * xs
    ss_lo = jnp.sum(jnp.where(lo, sq, 0.0), axis=-1, keepdims=True)
    ss_hi = jnp.sum(jnp.where(lo, 0.0, sq), axis=-1, keepdims=True)
    inv = jnp.where(lo, lax.rsqrt(ss_lo * (1.0 / HEAD_DIM) + EPS), lax.rsqrt(ss_hi * (1.0 / HEAD_DIM) + EPS))
    xn = xs * inv * gain
    return xn * rc + pltpu.roll(xn, ROPE_HALF, 1) * rsa + pltpu.roll(xn, LANES - ROPE_HALF, 1) * rsb


def _rows(ref, t0, n, seg, cols=slice(None)):
    if seg >= n:
        return ref[t0 // seg, t0 % seg:t0 % seg + n, cols]
    v = ref[t0 // seg:(t0 + n) // seg, :, cols]
    return v.reshape(n, v.shape[-1])


def _mixer_kernel(x_ref, kh_ref, vh_ref, uh_ref, rc_ref, rsa_ref, rsb_ref, sink_ref,
                  n1g_ref, win_ref, qg_ref, kg_ref, cw_ref, cb_ref, lng_ref, lnb_ref,
                  h_ref, kst_ref, vst_ref, ust_ref,
                  xn_buf, kctx, vctx, kbf, vdup, uctx, qhm, cpre, acb,
                  *, ns, seg, qb, br, carry, ninv_first):
    tm = ns * seg
    hdr = KEY_SPAN - qb
    pn = min(seg, br)
    j = pl.program_id(1)
    lo_lane = lax.broadcasted_iota(jnp.int32, (1, LANES), 1) < HEAD_DIM

    def dup_v(v, g):
        r = pltpu.roll(v, HEAD_DIM, 1)
        return jnp.where(lo_lane, v, r) if g == 0 else jnp.where(lo_lane, r, v)

    def load_history():
        kh = kh_ref[...]
        vh = vh_ref[...]
        kctx[:, hdr - WINDOW:hdr, :] = kh
        vctx[:, hdr - WINDOW:hdr, :] = vh
        kbf[:, hdr - WINDOW:hdr, :] = kh.astype(_BF16)
        vh2 = vh.reshape(ns * WINDOW, LANES)
        for g in range(N_KV_HEADS):
            vdup[g, :, hdr - WINDOW:hdr, :] = dup_v(vh2, g).reshape(ns, WINDOW, LANES).astype(_BF16)
        uctx[:, 0:CONV_HDR, :] = uh_ref[...]

    if carry:
        pl.when(j == 0)(load_history)
    else:
        load_history()
    if hdr > WINDOW:
        kbf[:, 0:hdr - WINDOW, :] = jnp.zeros((ns, hdr - WINDOW, LANES), _BF16)
        vdup[:, :, 0:hdr - WINDOW, :] = jnp.zeros((N_KV_HEADS, ns, hdr - WINDOW, LANES), _BF16)

    rows = GROUP * qb
    glu0 = ATTN_WIDTH + 2 * KV_COLS

    def chain_steps(blk):
        t0 = blk * br
        tr = slice(t0, t0 + br)
        pieces = [((t0 + i * pn) // seg, (t0 + i * pn) % seg) for i in range(br // pn)]
        probs = {}

        def rope_rows():
            return rc_ref[tr, :], rsa_ref[tr, :], rsb_ref[tr, :]

        def norm():
            nr = min(pn, NORM_ROWS)
            for r0 in range(0, br, nr):
                x = _rows(x_ref, t0 + r0, nr, seg)
                ms = jnp.mean(x * x, axis=-1, keepdims=True)
                xn_buf[r0:r0 + nr, :] = (x * lax.rsqrt(ms + EPS) * n1g_ref[...]).astype(_BF16)

        def keys_values():
            kv = _dot(xn_buf[...], win_ref[:, ATTN_WIDTH:ATTN_WIDTH + 2 * KV_COLS])
            k_new = _head_norm_rope(kv[:, 0:KV_COLS], kg_ref[...], *rope_rows())
            v_new = kv[:, KV_COLS:2 * KV_COLS]
            k_bf = k_new.astype(_BF16)
            v_dup = [dup_v(v_new, g).astype(_BF16) for g in range(N_KV_HEADS)]
            for i, (s, r) in enumerate(pieces):
                pr = slice(i * pn, (i + 1) * pn)
                kctx[s, hdr + r:hdr + r + pn, :] = k_new[pr]
                vctx[s, hdr + r:hdr + r + pn, :] = v_new[pr]
                kbf[s, hdr + r:hdr + r + pn, :] = k_bf[pr]
                for g in range(N_KV_HEADS):
                    vdup[g, s, hdr + r:hdr + r + pn, :] = v_dup[g][pr]

        def queries(nb):
            qq = _dot(xn_buf[...], win_ref[:, 256 * nb:256 * nb + 256])
            for half in range(2):
                m = 2 * nb + half
                g = (2 * m) // GROUP
                qr = _head_norm_rope(qq[:, LANES * half:LANES * (half + 1)], qg_ref[...], *rope_rows())
                qr = qr * (HEAD_DIM ** -0.5)
                sw = pltpu.roll(qr, HEAD_DIM, 1)
                if g == 0:
                    even, odd = jnp.where(lo_lane, qr, 0.0), jnp.where(lo_lane, sw, 0.0)
                else:
                    even, odd = jnp.where(lo_lane, 0.0, sw), jnp.where(lo_lane, 0.0, qr)
                qhm[2 * m, tr, :] = even.astype(_BF16)
                qhm[2 * m + 1, tr, :] = odd.astype(_BF16)

        def glu(nb):
            ga = _dot(xn_buf[...], win_ref[:, glu0 + 256 * nb:glu0 + 256 * nb + 256])
            gb = _dot(xn_buf[...], win_ref[:, glu0 + CONV_WIDTH + 256 * nb:glu0 + CONV_WIDTH + 256 * nb + 256])
            u = ga * jax.nn.sigmoid(gb)
            for i, (s, r) in enumerate(pieces):
                uctx[s, CONV_HDR + r:CONV_HDR + r + pn, 256 * nb:256 * nb + 256] = u[i * pn:(i + 1) * pn]

        def scores(i, b, g):
            s, r = pieces[i]
            q0 = t0 + i * pn + b * qb
            k0 = r + b * qb
            if carry:
                ninv = jnp.where(j * (seg // qb) + k0 // qb == 0, ninv_first, 0)
            else:
                ninv = ninv_first
            col = lax.broadcasted_iota(jnp.int32, (1, KEY_SPAN), 1)
            if qb == 2 * CHUNK:
                first_chunk = (lax.broadcasted_iota(jnp.int32, (rows, 1), 0) & (qb - 1)) < CHUNK
                first_key = jnp.where(first_chunk, ninv, jnp.maximum(ninv, CHUNK))
                end_key = jnp.where(first_chunk, KEY_SPAN - CHUNK, KEY_SPAN)
                valid = (col >= first_key) & (col < end_key)
            else:
                valid = jnp.broadcast_to(col, (rows, KEY_SPAN)) >= ninv
            q = qhm[GROUP * g:GROUP * (g + 1), q0:q0 + qb, :].reshape(rows, LANES)
            kc = kbf[s, k0:k0 + KEY_SPAN, :]
            sc = lax.dot_general(q, kc, (((1,), (1,)), ((), ())), preferred_element_type=_F32)
            sc = jnp.where(valid, sc, NEG_BIG)
            sink = sink_ref[g]
            mx = jnp.maximum(jnp.max(sc, axis=-1, keepdims=True), sink)
            p = jnp.exp(sc - mx)
            denom = jnp.sum(p, axis=-1, keepdims=True) + jnp.exp(sink - mx)
            probs[i, b, g] = (p.astype(_BF16), denom)

        def values(i, b, g):
            s, r = pieces[i]
            q0 = t0 + i * pn + b * qb
            k0 = r + b * qb
            p, denom = probs[i, b, g]
            o = _dot(p, vdup[g, s, k0:k0 + KEY_SPAN, :]) / denom
            for m4 in range(GROUP // 2):
                slab = jnp.where(lo_lane, o[(2 * m4) * qb:(2 * m4 + 1) * qb], o[(2 * m4 + 1) * qb:(2 * m4 + 2) * qb])
                c0 = LANES * ((GROUP // 2) * g + m4)
                acb[blk, q0 - t0:q0 - t0 + qb, c0:c0 + LANES] = slab.astype(_BF16)

        def conv(cv):
            lanes = slice(LANES * cv, LANES * (cv + 1))
            cn = min(pn, CONV_ROWS)
            for i, (s, r0) in enumerate(pieces):
                for sub in range(pn // cn):
                    r = r0 + sub * cn
                    acc = jnp.broadcast_to(cb_ref[:, lanes], (cn, LANES))
                    for back in range(SUBLANES):
                        offs = [o for o in range(CONV_HDR - CONV_K + 1, CONV_HDR + 1) if (o + back) % SUBLANES == 0]
                        lead = SUBLANES if back else 0
                        part = None
                        for o in offs:
                            a8 = o + back
                            term = uctx[s, r + a8 - lead:r + a8 + cn, lanes] * cw_ref[o - 2:o - 1, lanes]
                            part = term if part is None else part + term
                        acc = acc + part[lead - back:lead - back + cn]
                    c0 = i * pn + sub * cn
                    cpre[c0:c0 + cn, lanes] = acc

        def layer_norm_silu():
            cp = cpre[...]
            mu = jnp.mean(cp, axis=-1, keepdims=True)
            xc = cp - mu
            var = jnp.mean(xc * xc, axis=-1, keepdims=True)
            yn = xc * lax.rsqrt(var + EPS) * lng_ref[...] + lnb_ref[...]
            acb[blk, :, ATTN_WIDTH:D_MODEL] = (yn * jax.nn.sigmoid(yn)).astype(_BF16)

        def project_out(nb):
            cols = slice(OUT_COLS * nb, OUT_COLS * (nb + 1))
            wcols = slice(IN_COLS + OUT_COLS * nb, IN_COLS + OUT_COLS * (nb + 1))
            hv = _rows(x_ref, t0, br, seg, cols) + _dot(acb[blk], win_ref[:, wcols])
            if seg >= br:
                h_ref[t0 // seg, t0 % seg:t0 % seg + br, cols] = hv
            else:
                h_ref[t0 // seg:(t0 + br) // seg, :, cols] = hv.reshape(br // seg, seg, OUT_COLS)

        blocks = [(i, b, g) for i in range(len(pieces)) for b in range(pn // qb) for g in range(N_KV_HEADS)]
        return dict(
            first=[norm, keys_values] + [functools.partial(queries, nb) for nb in range(ATTN_WIDTH // 256)],
            glu=[functools.partial(glu, nb) for nb in range(CONV_WIDTH // 256)],
            scores=[functools.partial(scores, *k) for k in blocks],
            values=[functools.partial(values, *k) for k in blocks],
            vector=[functools.partial(conv, cv) for cv in range(CONV_WIDTH // LANES)] + [layer_norm_silu],
            out=[functools.partial(project_out, nb) for nb in range(D_MODEL // OUT_COLS)])

    nblk = tm // br
    steps = {}
    for it in range(nblk + 2):
        matmul_steps, vector_steps = [], []
        if it < nblk:
            steps[it] = chain_steps(it)
        if 0 <= it - 1 < nblk:
            matmul_steps += steps[it - 1]["scores"]
            vector_steps = steps[it - 1]["vector"]
        if 0 <= it - 2 < nblk:
            matmul_steps += steps.pop(it - 2)["out"]
        if it < nblk:
            matmul_steps += steps[it]["first"]
        if 0 <= it - 1 < nblk:
            matmul_steps += steps[it - 1]["values"]
        if it < nblk:
            matmul_steps += steps[it]["glu"]
        for n in range(max(len(matmul_steps), len(vector_steps))):
            if n < len(matmul_steps):
                matmul_steps[n]()
            if n < len(vector_steps):
                vector_steps[n]()

    kst_ref[...] = kctx[:, hdr + seg - WINDOW:hdr + seg, :]
    vst_ref[...] = vctx[:, hdr + seg - WINDOW:hdr + seg, :]
    ust_ref[...] = uctx[:, seg + CONV_HDR - (CONV_K - 1):seg + CONV_HDR, :]
    if carry:
        kctx[:, 0:WINDOW, :] = kctx[:, seg:seg + WINDOW, :]
        vctx[:, 0:WINDOW, :] = vctx[:, seg:seg + WINDOW, :]
        kbf[:, 0:WINDOW, :] = kbf[:, seg:seg + WINDOW, :]
        vdup[:, :, 0:WINDOW, :] = vdup[:, :, seg:seg + WINDOW, :]
        uctx[:, 0:CONV_HDR, :] = uctx[:, seg:seg + CONV_HDR, :]


def _const_spec(shape, single_buffer=False):
    zeros = (0,) * len(shape)
    if single_buffer:
        return pl.BlockSpec(shape, lambda b, j: zeros, pipeline_mode=pl.Buffered(1))
    return pl.BlockSpec(shape, lambda b, j: zeros)


def _mixer(x, khist, vhist, uhist, rope, sinks, w, *, ns, seg, qb, br, carry, ninv_first, hist_per_tile):
    n_seq, seq_len, _ = x.shape
    tm = ns * seg
    hdr = KEY_SPAN - qb
    grid = (n_seq // ns, seq_len // seg)
    hist_map = (lambda b, j: (b, 0, 0)) if hist_per_tile else (lambda b, j: (0, 0, 0))
    sink_cols = jnp.repeat(sinks.reshape(N_KV_HEADS, GROUP), qb, axis=1)[..., None].astype(_F32)
    rc, rsa, rsb = rope
    in_specs = [
        pl.BlockSpec((ns, seg, D_MODEL), lambda b, j: (b, j, 0)),
        pl.BlockSpec((ns, WINDOW, LANES), hist_map),
        pl.BlockSpec((ns, WINDOW, LANES), hist_map),
        pl.BlockSpec((ns, CONV_HDR, CONV_WIDTH), hist_map),
        pl.BlockSpec((tm, LANES), lambda b, j: (j, 0)),
        pl.BlockSpec((tm, LANES), lambda b, j: (j, 0)),
        pl.BlockSpec((tm, LANES), lambda b, j: (j, 0)),
        _const_spec((N_KV_HEADS, GROUP * qb, 1), single_buffer=True),
        _const_spec((1, D_MODEL)),
        _const_spec((D_MODEL, IN_COLS + D_MODEL), single_buffer=True),
        _const_spec((1, LANES)),
        _const_spec((1, LANES)),
        _const_spec((CONV_K, CONV_WIDTH)),
        _const_spec((1, CONV_WIDTH)),
        _const_spec((1, CONV_WIDTH)),
        _const_spec((1, CONV_WIDTH)),
    ]
    out_specs = [
        pl.BlockSpec((ns, seg, D_MODEL), lambda b, j: (b, j, 0)),
        pl.BlockSpec((ns, WINDOW, LANES), lambda b, j: (b, 0, 0)),
        pl.BlockSpec((ns, WINDOW, LANES), lambda b, j: (b, 0, 0)),
        pl.BlockSpec((ns, CONV_K - 1, CONV_WIDTH), lambda b, j: (b, 0, 0)),
    ]
    out_shape = [
        jax.ShapeDtypeStruct((n_seq, seq_len, D_MODEL), _F32),
        jax.ShapeDtypeStruct((n_seq, WINDOW, LANES), _F32),
        jax.ShapeDtypeStruct((n_seq, WINDOW, LANES), _F32),
        jax.ShapeDtypeStruct((n_seq, CONV_K - 1, CONV_WIDTH), _F32),
    ]
    scratch = [
        pltpu.VMEM((br, D_MODEL), _BF16),
        pltpu.VMEM((ns, hdr + seg, LANES), _F32),
        pltpu.VMEM((ns, hdr + seg, LANES), _F32),
        pltpu.VMEM((ns, hdr + seg, LANES), _BF16),
        pltpu.VMEM((N_KV_HEADS, ns, hdr + seg, LANES), _BF16),
        pltpu.VMEM((ns, CONV_HDR + seg, CONV_WIDTH), _F32),
        pltpu.VMEM((N_HEADS, tm, LANES), _BF16),
        pltpu.VMEM((br, CONV_WIDTH), _F32),
        pltpu.VMEM((tm // br, br, D_MODEL), _BF16),
    ]
    kern = functools.partial(_mixer_kernel, ns=ns, seg=seg, qb=qb, br=br, carry=carry, ninv_first=ninv_first)
    return pl.pallas_call(
        kern,
        grid=grid,
        in_specs=in_specs,
        out_specs=out_specs,
        out_shape=out_shape,
        scratch_shapes=scratch,
        compiler_params=pltpu.CompilerParams(
            dimension_semantics=("arbitrary", "arbitrary"), vmem_limit_bytes=VMEM_LIMIT),
        name=f"mixer_ns{ns}_seg{seg}",
    )(x, khist, vhist, uhist, rc, rsa, rsb, sink_cols,
      w["norm1_g"], w["w_mix"], w["q_gain"], w["k_gain"], w["conv_w"], w["conv_b"],
      w["conv_ln_g"], w["conv_ln_b"])


def _ffn_kernel(h_ref, fh_ref, n2g_ref, wg_ref, wu_ref, fcw_ref, fcb_ref, wd_ref,
                y_ref, fst_ref, hn_buf, g_buf, carry_buf, *, ns, seg, cr, carry):
    tm = ns * seg
    j = pl.program_id(1)
    c = pl.program_id(2)

    @pl.when(c == 0)
    def _():
        nr = min(seg, NORM_ROWS)
        for t0 in range(0, tm, nr):
            h = _rows(h_ref, t0, nr, seg)
            ms = jnp.mean(h * h, axis=-1, keepdims=True)
            hn_buf[t0:t0 + nr, :] = (h * lax.rsqrt(ms + EPS) * n2g_ref[...]).astype(_BF16)
            y_ref[t0 // seg, t0 % seg:t0 % seg + nr, :] = h

    if carry:
        @pl.when(j == 0)
        def _():
            carry_buf[c, FFN_HDR - 2:FFN_HDR, :] = fh_ref[0]

        g_buf[FFN_HDR - 2:FFN_HDR, :] = carry_buf[c, FFN_HDR - 2:FFN_HDR, :]
    else:
        g_buf[FFN_HDR - 2:FFN_HDR, :] = fh_ref[0]

    def project(r):
        t0 = r * cr
        hn = hn_buf[t0:t0 + cr, :]
        gp = _dot(hn, wg_ref[...])
        up = _dot(hn, wu_ref[...])
        g_buf[FFN_HDR + t0:FFN_HDR + t0 + cr, :] = gp
        return gp, up

    def finish(r, gp, up):
        t0 = r * cr
        s1 = g_buf[FFN_HDR + t0 - 1:FFN_HDR + t0 - 1 + cr, :]
        s2 = g_buf[FFN_HDR + t0 - 2:FFN_HDR + t0 - 2 + cr, :]
        if seg < cr:
            nsc = cr // seg
            fh = fh_ref[t0 // seg:t0 // seg + nsc]
            h0 = fh[:, 0:1, :]
            h1 = fh[:, 1:2, :]
            rid = lax.broadcasted_iota(jnp.int32, (nsc, seg, FF_BLOCK), 1)
            s1 = jnp.where(rid == 0, h1, s1.reshape(nsc, seg, FF_BLOCK)).reshape(cr, FF_BLOCK)
            s2 = jnp.where(rid == 0, h0, jnp.where(rid == 1, h1, s2.reshape(nsc, seg, FF_BLOCK))).reshape(cr, FF_BLOCK)
        gc = fcw_ref[0:1, :] * s2 + fcw_ref[1:2, :] * s1 + fcw_ref[2:3, :] * gp + fcb_ref[...]
        act = (gc * jax.nn.sigmoid(gc)) * up
        down = _dot(act.astype(_BF16), wd_ref[...])
        if seg >= cr:
            y_ref[t0 // seg, t0 % seg:t0 % seg + cr, :] += down
        else:
            y_ref[t0 // seg:(t0 + cr) // seg, :, :] += down.reshape(cr // seg, seg, D_MODEL)

    nchain = tm // cr
    nxt = project(0)
    for r in range(nchain):
        cur = nxt
        if r + 1 < nchain:
            nxt = project(r + 1)
        finish(r, *cur)

    last = g_buf[FFN_HDR:FFN_HDR + tm, :].reshape(ns, seg, FF_BLOCK)[:, seg - (FFN_CONV_K - 1):seg, :]
    fst_ref[...] = last[:, None]
    if carry:
        carry_buf[c, FFN_HDR - 2:FFN_HDR, :] = g_buf[FFN_HDR + tm - 2:FFN_HDR + tm, :]


def _ffn(h, fhist, w, *, ns, seg, cr, carry, hist_per_tile):
    n_seq, seq_len, _ = h.shape
    tm = ns * seg
    n_ff = D_FF // FF_BLOCK
    grid = (n_seq // ns, seq_len // seg, n_ff)
    fh_map = (lambda b, j, c: (b, 0, c)) if hist_per_tile else (lambda b, j, c: (0, 0, c))
    in_specs = [
        pl.BlockSpec((ns, seg, D_MODEL), lambda b, j, c: (b, j, 0)),
        pl.BlockSpec((ns, FFN_CONV_K - 1, FF_BLOCK), fh_map),
        pl.BlockSpec((1, D_MODEL), lambda b, j, c: (0, 0)),
        pl.BlockSpec((D_MODEL, FF_BLOCK), lambda b, j, c: (0, c)),
        pl.BlockSpec((D_MODEL, FF_BLOCK), lambda b, j, c: (0, c)),
        pl.BlockSpec((FFN_CONV_K, FF_BLOCK), lambda b, j, c: (0, c)),
        pl.BlockSpec((1, FF_BLOCK), lambda b, j, c: (0, c)),
        pl.BlockSpec((FF_BLOCK, D_MODEL), lambda b, j, c: (c, 0)),
    ]
    out_specs = [
        pl.BlockSpec((ns, seg, D_MODEL), lambda b, j, c: (b, j, 0)),
        pl.BlockSpec((ns, 1, FFN_CONV_K - 1, FF_BLOCK), lambda b, j, c: (b, j, 0, c)),
    ]
    out_shape = [
        jax.ShapeDtypeStruct((n_seq, seq_len, D_MODEL), _F32),
        jax.ShapeDtypeStruct((n_seq, seq_len // seg, FFN_CONV_K - 1, D_FF), _F32),
    ]
    scratch = [
        pltpu.VMEM((tm, D_MODEL), _BF16),
        pltpu.VMEM((FFN_HDR + tm, FF_BLOCK), _F32),
        pltpu.VMEM((n_ff, FFN_HDR, FF_BLOCK), _F32),
    ]
    kern = functools.partial(_ffn_kernel, ns=ns, seg=seg, cr=cr, carry=carry)
    return pl.pallas_call(
        kern,
        grid=grid,
        in_specs=in_specs,
        out_specs=out_specs,
        out_shape=out_shape,
        scratch_shapes=scratch,
        compiler_params=pltpu.CompilerParams(
            dimension_semantics=("arbitrary", "arbitrary", "arbitrary"), vmem_limit_bytes=VMEM_LIMIT),
        name=f"ffn_ns{ns}_seg{seg}",
    )(h, fhist, w["norm2_g"], w["w_gate"], w["w_up"], w["ffn_conv_w"], w["ffn_conv_b"], w["w_down"])


def _gate_rows_kernel(h_ref, n2g_ref, wg_ref, o_ref):
    h = h_ref[...]
    ms = jnp.mean(h * h, axis=-1, keepdims=True)
    hn = (h * lax.rsqrt(ms + EPS) * n2g_ref[...]).astype(_BF16)
    o_ref[...] = _dot(hn, wg_ref[...])


def _gate_rows(h, w):
    n = h.shape[0]
    return pl.pallas_call(
        _gate_rows_kernel,
        grid=(D_FF // FF_BLOCK,),
        in_specs=[
            pl.BlockSpec((n, D_MODEL), lambda c: (0, 0)),
            pl.BlockSpec((1, D_MODEL), lambda c: (0, 0)),
            pl.BlockSpec((D_MODEL, FF_BLOCK), lambda c: (0, c)),
        ],
        out_specs=pl.BlockSpec((n, FF_BLOCK), lambda c: (0, c)),
        out_shape=jax.ShapeDtypeStruct((n, D_FF), _F32),
        compiler_params=pltpu.CompilerParams(dimension_semantics=("arbitrary",)),
        name="gate_rows",
    )(h, w["norm2_g"], w["w_gate"])


def _rope_tables(pos, reps):
    inv_freq = ROPE_THETA ** (-jnp.arange(ROPE_HALF, dtype=_F32) / ROPE_HALF)
    ang = pos.astype(_F32)[:, None] * inv_freq[None, :]
    cos, sin = jnp.cos(ang), jnp.sin(ang)
    n = pos.shape[0]
    rest = HEAD_DIM - 2 * ROPE_HALF
    c = jnp.concatenate([cos, cos, jnp.ones((n, rest), _F32)], axis=1)
    sa = jnp.concatenate([jnp.zeros((n, ROPE_HALF), _F32), sin, jnp.zeros((n, rest), _F32)], axis=1)
    sb = jnp.concatenate([-sin, jnp.zeros((n, ROPE_HALF + rest), _F32)], axis=1)
    return tuple(jnp.tile(t, (reps, LANES // HEAD_DIM)) for t in (c, sa, sb))


def kernel(x_prompt, x_sample, cache_k, cache_v, state_conv, state_ffn_conv, meta_tokens, norm1_g, w_in,
           q_norm_g, k_norm_g, sinks, conv_w, conv_b, conv_ln_g, conv_ln_b, w_out, norm2_g, w_gate, w_up,
           ffn_conv_w, ffn_conv_b, w_down):
    assert w_in.shape[0] == 1, "single-layer problem"
    bsz, seq_len, _ = x_prompt.shape
    dbsz, dseq, _ = x_sample.shape
    lane_pair = LANES // HEAD_DIM
    w = {
        "norm1_g": norm1_g[0][None],
        "w_mix": jnp.concatenate([w_in[0], w_out[0]], axis=1).astype(_BF16),
        "q_gain": jnp.tile(q_norm_g[0], lane_pair)[None],
        "k_gain": jnp.tile(k_norm_g[0], lane_pair)[None],
        "conv_w": conv_w[0],
        "conv_b": conv_b[0][None],
        "conv_ln_g": conv_ln_g[0][None],
        "conv_ln_b": conv_ln_b[0][None],
        "norm2_g": norm2_g[0][None],
        "w_gate": w_gate[0].astype(_BF16),
        "w_up": w_up[0].astype(_BF16),
        "ffn_conv_w": ffn_conv_w[0],
        "ffn_conv_b": ffn_conv_b[0][None],
        "w_down": w_down[0].astype(_BF16),
    }
    sink = sinks[0]

    zk = jnp.zeros((1, WINDOW, LANES), _F32)
    zu = jnp.zeros((1, CONV_HDR, CONV_WIDTH), _F32)
    h_m, k_m, v_m, u_m = _mixer(
        meta_tokens[None], zk, zk, zu, _rope_tables(jnp.arange(N_META, dtype=jnp.int32), 1), sink, w,
        ns=1, seg=N_META, qb=N_META, br=N_META, carry=False, ninv_first=KEY_SPAN - N_META, hist_per_tile=False)
    gate_m = _gate_rows(h_m[0], w)
    fhist_p = gate_m[N_META - (FFN_CONV_K - 1):][None]
    uhist_p = jnp.pad(u_m, ((0, 0), (CONV_HDR - (CONV_K - 1), 0), (0, 0)))

    rope_p = _rope_tables(N_META + jnp.arange(seq_len, dtype=jnp.int32), 1)
    h_p, k_p, v_p, u_p = _mixer(
        x_prompt, k_m, v_m, uhist_p, rope_p, sink, w,
        ns=1, seg=256, qb=2 * CHUNK, br=128, carry=True, ninv_first=WINDOW - N_META, hist_per_tile=False)
    y_p, f_p = _ffn(h_p, fhist_p, w, ns=1, seg=1024, cr=256, carry=True, hist_per_tile=False)

    ns_s = 4
    rope_s = _rope_tables(N_META + PAST_LEN + jnp.arange(dseq, dtype=jnp.int32), ns_s)
    uhist_s = jnp.pad(state_conv[0], ((0, 0), (CONV_HDR - (CONV_K - 1), 0), (0, 0)))
    h_s, k_s, v_s, u_s = _mixer(
        x_sample, cache_k[0].reshape(dbsz, WINDOW, LANES), cache_v[0].reshape(dbsz, WINDOW, LANES), uhist_s,
        rope_s, sink, w,
        ns=ns_s, seg=dseq, qb=dseq, br=256, carry=False, ninv_first=KEY_SPAN - WINDOW - dseq, hist_per_tile=True)
    y_s, f_s = _ffn(h_s, state_ffn_conv[0], w, ns=8, seg=dseq, cr=256, carry=False, hist_per_tile=True)

    kv_shape_p = (1, bsz, WINDOW, N_KV_HEADS, HEAD_DIM)
    kv_shape_s = (1, dbsz, WINDOW, N_KV_HEADS, HEAD_DIM)
    return (y_p, y_s, k_p.reshape(kv_shape_p), v_p.reshape(kv_shape_p), u_p[None], f_p[None, :, -1],
            k_s.reshape(kv_shape_s), v_s.reshape(kv_shape_s), u_s[None], f_s[None, :, -1])
```
